```python
import math
import jax
import jax.numpy as jnp
from jax import lax
import numpy as np

D_MODEL = 4096
BATCH = 4
SEQ = 2048
DEPTH = 4
DEC_BATCH = 128
DEC_SEQ = 8
PAST_LEN = 8192
PAGE_SIZE = 128

N_MIXERS = 3
MIXER_OF_LAYER = tuple(l % N_MIXERS for l in range(DEPTH))
N_A = MIXER_OF_LAYER.count(0)
N_B = MIXER_OF_LAYER.count(1)
N_C = MIXER_OF_LAYER.count(2)

EPS = 1e-6
NEG_INF = -1e30
Q_BLOCK = 128
ADA_CHUNKS = 6

A_HEADS = 16
A_HEAD_DIM = 128
A_KV_HEADS = 1
A_V_DIM = 2 * A_HEAD_DIM
A_Q_COLS = A_HEADS * 2 * A_HEAD_DIM
A_K_COLS = A_KV_HEADS * 2 * A_HEAD_DIM
A_V_COLS = A_KV_HEADS * A_V_DIM

B_HEADS = 8
B_QK_DIM = D_MODEL // B_HEADS // 2
B_V_DIM = D_MODEL // B_HEADS
B_CHUNK = 64
B_GATE_CAP = 15.0

C_HEADS = 64
C_Q_LORA = 1536
C_KV_LORA = 512
C_NOPE = 128
C_ROPE = 64
C_V_DIM = 128
ROPE_THETA = 10000.0

D_FF = 11008
CONV_W = 3

N_PAGES = PAST_LEN // PAGE_SIZE
N_POOL = (DEC_BATCH * N_PAGES * 5) // 4

kernel_name = 'hybrid_diffattn_mlstm_mla_convffn_step'


def rmsnorm(x, g):
    xf = x.astype(jnp.float32)
    y = xf * lax.rsqrt(jnp.mean(jnp.square(xf), axis=-1, keepdims=True) + EPS)
    return (y * g.astype(jnp.float32)).astype(x.dtype)


def rope(x, pos):
    half = x.shape[-1] // 2
    freqs = ROPE_THETA ** (-jnp.arange(half, dtype=jnp.float32) / half)
    ang = pos.astype(jnp.float32)[:, None] * freqs[None, :]
    cos = jnp.cos(ang)[:, None, :]
    sin = jnp.sin(ang)[:, None, :]
    xf = x.astype(jnp.float32)
    x1, x2 = xf[..., :half], xf[..., half:]
    return jnp.concatenate([x1 * cos - x2 * sin, x2 * cos + x1 * sin], axis=-1).astype(x.dtype)


def block_causal_mask(blk, n_keys):
    qpos = blk * Q_BLOCK + jnp.arange(Q_BLOCK)
    return qpos[:, None] >= jnp.arange(n_keys)[None, :]


def sample_mask(n_past, n_new):
    return jnp.concatenate([jnp.ones((n_new, n_past), dtype=bool),
                            jnp.tril(jnp.ones((n_new, n_new), dtype=bool))], axis=1)


def to_query_blocks(t):
    b, s = t.shape[:2]
    return jnp.moveaxis(t.reshape((b, s // Q_BLOCK, Q_BLOCK) + t.shape[2:]), 1, 0)


def from_query_blocks(o):
    nb, b, qb = o.shape[:3]
    return jnp.moveaxis(o, 0, 1).reshape((b, nb * qb) + o.shape[3:])


def ada_base(c, w_ada, b_ada):
    return (jax.nn.silu(c) @ w_ada + b_ada).reshape(c.shape[0], ADA_CHUNKS, D_MODEL)


def modulate(h, shift, scale):
    return h * (1.0 + scale[:, None, :]) + shift[:, None, :]


def diff_lambda_init(layer):
    return 0.8 - 0.6 * math.exp(-0.3 * layer)


def diff_lambda(lam_p, lam_init):
    lp = lam_p.astype(jnp.float32)
    return jnp.exp(jnp.sum(lp[0] * lp[1])) - jnp.exp(jnp.sum(lp[2] * lp[3])) + lam_init


def diff_attn_project(h, w_qkv):
    b, t, _ = h.shape
    z = h @ w_qkv
    q = z[..., :A_Q_COLS].reshape(b, t, A_HEADS, 2, A_HEAD_DIM)
    k = z[..., A_Q_COLS:A_Q_COLS + A_K_COLS].reshape(b, t, A_KV_HEADS, 2 * A_HEAD_DIM)
    v = z[..., A_Q_COLS + A_K_COLS:].reshape(b, t, A_KV_HEADS, A_V_DIM)
    return q, k, v


def diff_attn_core(q, k, v, mask, lam, lam_init, g_sub):
    b, tq = q.shape[:2]
    tk = k.shape[1]
    grp = A_HEADS // A_KV_HEADS
    qg = q.reshape(b, tq, A_KV_HEADS, grp, 2, A_HEAD_DIM)
    kc = k.reshape(b, tk, A_KV_HEADS, 2, A_HEAD_DIM)
    s = jnp.einsum('bqkgcd,bskcd->bkgcqs', qg, kc).astype(jnp.float32) * (A_HEAD_DIM ** -0.5)
    p = jax.nn.softmax(jnp.where(mask, s, NEG_INF), axis=-1)
    a = (p[:, :, :, 0] - lam * p[:, :, :, 1]).astype(v.dtype)
    o = jnp.einsum('bkgqs,bskv->bqkgv', a, v).reshape(b, tq, A_HEADS, A_V_DIM)
    o = rmsnorm(o, g_sub) * (1.0 - lam_init)
    return o.reshape(b, tq, A_HEADS * A_V_DIM)


def diff_attn_prompt(h, w_qkv, w_o, lam, lam_init, g_sub):
    q, k, v = diff_attn_project(h, w_qkv)
    s = h.shape[1]

    def block(args):
        q_blk, idx = args
        return diff_attn_core(q_blk, k, v, block_causal_mask(idx, s), lam, lam_init, g_sub)

    o = lax.map(block, (to_query_blocks(q), jnp.arange(s // Q_BLOCK)))
    return from_query_blocks(o) @ w_o, k, v


def diff_attn_sample(h, cache_k, cache_v, j, page_table, w_qkv, w_o, lam, lam_init, g_sub):
    q, k, v = diff_attn_project(h, w_qkv)
    mask = sample_mask(PAST_LEN, h.shape[1])

    def one(args):
        q1, k1, v1, pt = args
        kp = cache_k[j, pt].reshape(PAST_LEN, A_KV_HEADS, 2 * A_HEAD_DIM).astype(k1.dtype)
        vp = cache_v[j, pt].reshape(PAST_LEN, A_KV_HEADS, A_V_DIM).astype(v1.dtype)
        kall = jnp.concatenate([kp, k1], axis=0)[None]
        vall = jnp.concatenate([vp, v1], axis=0)[None]
        return diff_attn_core(q1[None], kall, vall, mask, lam, lam_init, g_sub)[0]

    o = lax.map(one, (q, k, v, page_table))
    return o @ w_o, k, v


def softcap(x):
    return B_GATE_CAP * jnp.tanh(x / B_GATE_CAP)


def mlstm_project(h, w_in, gate_bias):
    b, t, _ = h.shape
    z = h @ w_in
    nqk = B_HEADS * B_QK_DIM
    nv = B_HEADS * B_V_DIM
    zq, zk, zv, zo, zi, zf = jnp.split(
        z, [nqk, 2 * nqk, 2 * nqk + nv, 2 * nqk + 2 * nv, 2 * nqk + 2 * nv + B_HEADS], axis=-1)

    def heads(u, d):
        return jnp.swapaxes(u.reshape(b, t, B_HEADS, d), 1, 2).astype(jnp.float32)

    q = heads(zq, B_QK_DIM)
    k = heads(zk, B_QK_DIM) * (B_QK_DIM ** -0.5)
    v = heads(zv, B_V_DIM)
    gb = gate_bias.astype(jnp.float32)
    ig = jnp.swapaxes(softcap(zi.astype(jnp.float32) + gb[0]), 1, 2)
    lf = jnp.swapaxes(jax.nn.log_sigmoid(softcap(zf.astype(jnp.float32) + gb[1])), 1, 2)
    return q, k, v, ig, lf, zo


def mlstm_chunk(carry, inp):
    C, n, m = carry
    q, k, v, ig, lf = inp
    L = q.shape[2]
    bcum = jnp.cumsum(lf, axis=-1)
    causal = jnp.tril(jnp.ones((L, L), dtype=bool))
    log_w = jnp.where(causal, bcum[..., :, None] - bcum[..., None, :] + ig[..., None, :], -jnp.inf)
    log_inter = bcum + m[..., None]
    m_t = jnp.maximum(log_inter, jnp.max(log_w, axis=-1))
    s = jnp.einsum('bhtd,bhsd->bhts', q, k) * jnp.exp(log_w - m_t[..., None])
    a_inter = jnp.exp(log_inter - m_t)
    num = jnp.einsum('bhts,bhsv->bhtv', s, v) + a_inter[..., None] * jnp.einsum('bhtd,bhdv->bhtv', q, C)
    den = jnp.sum(s, axis=-1) + a_inter * jnp.einsum('bhtd,bhd->bht', q, n)
    h = num / jnp.maximum(jnp.abs(den), jnp.exp(-m_t))[..., None]
    m_new = m_t[..., -1]
    decay = jnp.exp(bcum[..., -1] + m - m_new)
    w_in = jnp.exp(bcum[..., -1:] - bcum + ig - m_new[..., None])
    C_new = decay[..., None, None] * C + jnp.einsum('bhsd,bhsv->bhdv', w_in[..., None] * k, v)
    n_new = decay[..., None] * n + jnp.einsum('bhs,bhsd->bhd', w_in, k)
    return (C_new, n_new, m_new), h


def mlstm_output(hcell, zo, g_out, w_out, dtype):
    b, nh, t, dv = hcell.shape
    hn = rmsnorm(hcell, g_out[:, None, :])
    hn = jnp.swapaxes(hn, 1, 2).reshape(b, t, nh * dv)
    y = (hn * jax.nn.sigmoid(zo.astype(jnp.float32))).astype(dtype)
    return y @ w_out


def mlstm_prompt(h, w_in, gate_bias, g_out, w_out):
    q, k, v, ig, lf, zo = mlstm_project(h, w_in, gate_bias)
    b, nh, s, _ = q.shape
    nc = s // B_CHUNK

    def chunks(u):
        return jnp.moveaxis(u.reshape(u.shape[:2] + (nc, B_CHUNK) + u.shape[3:]), 2, 0)

    init = (jnp.zeros((b, nh, B_QK_DIM, B_V_DIM), jnp.float32),
            jnp.zeros((b, nh, B_QK_DIM), jnp.float32),
            jnp.zeros((b, nh), jnp.float32))
    (C, n, m), hs = lax.scan(mlstm_chunk, init, (chunks(q), chunks(k), chunks(v), chunks(ig), chunks(lf)))
    hcell = jnp.moveaxis(hs, 0, 2).reshape(b, nh, s, B_V_DIM)
    return mlstm_output(hcell, zo, g_out, w_out, h.dtype), C, n, m


def mlstm_sample(h, C0, n0, m0, w_in, gate_bias, g_out, w_out):
    q, k, v, ig, lf, zo = mlstm_project(h, w_in, gate_bias)
    carry = (C0.astype(jnp.float32), n0.astype(jnp.float32), m0.astype(jnp.float32))
    (C, n, m), hcell = mlstm_chunk(carry, (q, k, v, ig, lf))
    return mlstm_output(hcell, zo, g_out, w_out, h.dtype), C, n, m


def mla_project(h, pos, w_in, g_q, g_kv, w_qb):
    b, t, _ = h.shape
    z = h @ w_in
    cq = rmsnorm(z[..., :C_Q_LORA], g_q)
    ckv = rmsnorm(z[..., C_Q_LORA:C_Q_LORA + C_KV_LORA], g_kv)
    kpe = rope(z[..., C_Q_LORA + C_KV_LORA:][:, :, None, :], pos)[:, :, 0, :]
    q = (cq @ w_qb).reshape(b, t, C_HEADS, C_NOPE + C_ROPE)
    return q[..., :C_NOPE], rope(q[..., C_NOPE:], pos), ckv, kpe


def mla_core(q_nope, q_pe, ckv, kpe, mask, w_kvb):
    b, tq = q_nope.shape[:2]
    w_uk = w_kvb[..., :C_NOPE]
    w_uv = w_kvb[..., C_NOPE:]
    q_lat = jnp.einsum('bqhd,chd->bqhc', q_nope, w_uk)
    s = (jnp.einsum('bqhc,bkc->bhqk', q_lat, ckv)
         + jnp.einsum('bqhr,bkr->bhqk', q_pe, kpe)).astype(jnp.float32) * ((C_NOPE + C_ROPE) ** -0.5)
    p = jax.nn.softmax(jnp.where(mask, s, NEG_INF), axis=-1)
    o_lat = jnp.einsum('bhqk,bkc->bqhc', p.astype(ckv.dtype), ckv)
    o = jnp.einsum('bqhc,chv->bqhv', o_lat, w_uv)
    return o.reshape(b, tq, C_HEADS * C_V_DIM)


def mla_prompt(h, pos, w_in, g_q, g_kv, w_qb, w_kvb, w_o):
    q_nope, q_pe, ckv, kpe = mla_project(h, pos, w_in, g_q, g_kv, w_qb)
    s = h.shape[1]

    def block(args):
        qn, qp, idx = args
        return mla_core(qn, qp, ckv, kpe, block_causal_mask(idx, s), w_kvb)

    o = lax.map(block, (to_query_blocks(q_nope), to_query_blocks(q_pe), jnp.arange(s // Q_BLOCK)))
    return from_query_blocks(o) @ w_o, ckv, kpe


def mla_sample(h, pos, cache_ckv, cache_kpe, j, page_table, w_in, g_q, g_kv, w_qb, w_kvb, w_o):
    q_nope, q_pe, ckv, kpe = mla_project(h, pos, w_in, g_q, g_kv, w_qb)
    mask = sample_mask(PAST_LEN, h.shape[1])

    def one(args):
        qn, qp, c1, r1, pt = args
        cp = cache_ckv[j, pt].reshape(PAST_LEN, C_KV_LORA).astype(c1.dtype)
        rp = cache_kpe[j, pt].reshape(PAST_LEN, C_ROPE).astype(r1.dtype)
        call = jnp.concatenate([cp, c1], axis=0)[None]
        rall = jnp.concatenate([rp, r1], axis=0)[None]
        return mla_core(qn[None], qp[None], call, rall, mask, w_kvb)[0]

    o = lax.map(one, (q_nope, q_pe, ckv, kpe, page_table))
    return o @ w_o, ckv, kpe


def conv_ffn(h, prev, w_up, conv_w, conv_b, w_down):
    t = h.shape[1]
    z = h @ w_up
    a, u = z[..., :D_FF], z[..., D_FF:]
    p = jnp.concatenate([prev.astype(a.dtype), a], axis=1)
    ac = conv_b
    for tap in range(CONV_W):
        ac = ac + conv_w[tap] * p[:, tap:tap + t]
    y = (jax.nn.silu(ac) * u) @ w_down
    return y, p[:, t:]


def setup_inputs(seed: int = 0) -> dict:
    key = jax.random.key(seed)
    keys = iter(jax.random.split(key, 64))

    def nrm(shape, scale=1.0):
        return jax.random.normal(next(keys), shape, jnp.float32) * scale

    def gain(shape):
        return 1.0 + nrm(shape, 0.02)

    n_used = DEC_BATCH * N_PAGES
    page_table = jax.random.permutation(next(keys), N_POOL)[:n_used].reshape(DEC_BATCH, N_PAGES).astype(jnp.int32)
    a_cols = A_Q_COLS + A_K_COLS + A_V_COLS
    b_cols = 2 * B_HEADS * B_QK_DIM + 2 * B_HEADS * B_V_DIM + 2 * B_HEADS
    c_cols = C_Q_LORA + C_KV_LORA + C_ROPE
    return dict(
        x_prompt=nrm((BATCH, SEQ, D_MODEL)),
        x_sample=nrm((DEC_BATCH, DEC_SEQ, D_MODEL)),
        cache_k_a=nrm((N_A, N_POOL, PAGE_SIZE, A_KV_HEADS, 2 * A_HEAD_DIM)),
        cache_v_a=nrm((N_A, N_POOL, PAGE_SIZE, A_KV_HEADS, A_V_DIM)),
        state_C_b=nrm((N_B, DEC_BATCH, B_HEADS, B_QK_DIM, B_V_DIM), 0.5),
        state_n_b=nrm((N_B, DEC_BATCH, B_HEADS, B_QK_DIM), 0.5),
        state_m_b=nrm((N_B, DEC_BATCH, B_HEADS)),
        cache_ckv_c=nrm((N_C, N_POOL, PAGE_SIZE, C_KV_LORA)),
        cache_kpe_c=nrm((N_C, N_POOL, PAGE_SIZE, C_ROPE)),
        state_ffn_conv=nrm((DEPTH, DEC_BATCH, CONV_W - 1, D_FF)),
        page_table=page_table,
        c_prompt=nrm((BATCH, D_MODEL)),
        c_sample=nrm((DEC_BATCH, D_MODEL)),
        w_ada=nrm((D_MODEL, ADA_CHUNKS * D_MODEL), D_MODEL ** -0.5),
        b_ada=nrm((ADA_CHUNKS * D_MODEL,), 0.02),
        ada_table=nrm((DEPTH, ADA_CHUNKS, D_MODEL), 0.1),
        g_mix=gain((DEPTH, D_MODEL)),
        g_ffn=gain((DEPTH, D_MODEL)),
        g_final=gain((D_MODEL,)),
        a_w_qkv=nrm((N_A, D_MODEL, a_cols), D_MODEL ** -0.5),
        a_lambda=nrm((N_A, 4, A_HEAD_DIM), 0.1),
        a_g_sub=gain((N_A, A_V_DIM)),
        a_w_o=nrm((N_A, A_HEADS * A_V_DIM, D_MODEL), (A_HEADS * A_V_DIM) ** -0.5),
        b_w_in=nrm((N_B, D_MODEL, b_cols), D_MODEL ** -0.5),
        b_gate_bias=jnp.stack([nrm((N_B, B_HEADS), 0.1), 3.0 + nrm((N_B, B_HEADS), 0.5)], axis=1),
        b_g_out=gain((N_B, B_HEADS, B_V_DIM)),
        b_w_out=nrm((N_B, B_HEADS * B_V_DIM, D_MODEL), (B_HEADS * B_V_DIM) ** -0.5),
        c_w_in=nrm((N_C, D_MODEL, c_cols), D_MODEL ** -0.5),
        c_g_q=gain((N_C, C_Q_LORA)),
        c_g_kv=gain((N_C, C_KV_LORA)),
        c_w_qb=nrm((N_C, C_Q_LORA, C_HEADS * (C_NOPE + C_ROPE)), C_Q_LORA ** -0.5),
        c_w_kvb=nrm((N_C, C_KV_LORA, C_HEADS, C_NOPE + C_V_DIM), C_KV_LORA ** -0.5),
        c_w_o=nrm((N_C, C_HEADS * C_V_DIM, D_MODEL), (C_HEADS * C_V_DIM) ** -0.5),
        f_w_up=nrm((DEPTH, D_MODEL, 2 * D_FF), D_MODEL ** -0.5),
        f_conv_w=nrm((DEPTH, CONV_W, D_FF), CONV_W ** -0.5),
        f_conv_b=nrm((DEPTH, D_FF), 0.02),
        f_w_down=nrm((DEPTH, D_FF, D_MODEL), D_FF ** -0.5),
    )


def reference(x_prompt, x_sample, cache_k_a, cache_v_a, state_C_b, state_n_b, state_m_b,
              cache_ckv_c, cache_kpe_c, state_ffn_conv, page_table, c_prompt, c_sample,
              w_ada, b_ada, ada_table, g_mix, g_ffn, g_final,
              a_w_qkv, a_lambda, a_g_sub, a_w_o,
              b_w_in, b_gate_bias, b_g_out, b_w_out,
              c_w_in, c_g_q, c_g_kv, c_w_qb, c_w_kvb, c_w_o,
              f_w_up, f_conv_w, f_conv_b, f_w_down):
    pos_p = jnp.arange(x_prompt.shape[1], dtype=jnp.int32)
    pos_s = PAST_LEN + jnp.arange(x_sample.shape[1], dtype=jnp.int32)
    base_p = ada_base(c_prompt, w_ada, b_ada)
    base_s = ada_base(c_sample, w_ada, b_ada)
    xp, xs = x_prompt, x_sample
    ka_p, ka_s, va_p, va_s = [], [], [], []
    cb_p, cb_s, nb_p, nb_s, mb_p, mb_s = [], [], [], [], [], []
    ckv_p, ckv_s, kpe_p, kpe_s = [], [], [], []
    cv_p, cv_s = [], []
    for l in range(DEPTH):
        kind, j = MIXER_OF_LAYER[l], l // N_MIXERS
        mod_p = base_p + ada_table[l]
        mod_s = base_s + ada_table[l]
        hp = modulate(rmsnorm(xp, g_mix[l]), mod_p[:, 0], mod_p[:, 1])
        hs = modulate(rmsnorm(xs, g_mix[l]), mod_s[:, 0], mod_s[:, 1])
        if kind == 0:
            lam_init = diff_lambda_init(l)
            lam = diff_lambda(a_lambda[j], lam_init)
            op, k_p, v_p = diff_attn_prompt(hp, a_w_qkv[j], a_w_o[j], lam, lam_init, a_g_sub[j])
            osm, k_s, v_s = diff_attn_sample(hs, cache_k_a, cache_v_a, j, page_table,
                                             a_w_qkv[j], a_w_o[j], lam, lam_init, a_g_sub[j])
            ka_p.append(k_p)
            ka_s.append(k_s)
            va_p.append(v_p)
            va_s.append(v_s)
        elif kind == 1:
            op, C_p, n_p, m_p = mlstm_prompt(hp, b_w_in[j], b_gate_bias[j], b_g_out[j], b_w_out[j])
            osm, C_s, n_s, m_s = mlstm_sample(hs, state_C_b[j], state_n_b[j], state_m_b[j],
                                              b_w_in[j], b_gate_bias[j], b_g_out[j], b_w_out[j])
            cb_p.append(C_p)
            cb_s.append(C_s)
            nb_p.append(n_p)
            nb_s.append(n_s)
            mb_p.append(m_p)
            mb_s.append(m_s)
        else:
            op, c_p, r_p = mla_prompt(hp, pos_p, c_w_in[j], c_g_q[j], c_g_kv[j], c_w_qb[j], c_w_kvb[j], c_w_o[j])
            osm, c_s, r_s = mla_sample(hs, pos_s, cache_ckv_c, cache_kpe_c, j, page_table,
                                       c_w_in[j], c_g_q[j], c_g_kv[j], c_w_qb[j], c_w_kvb[j], c_w_o[j])
            ckv_p.append(c_p)
            ckv_s.append(c_s)
            kpe_p.append(r_p)
            kpe_s.append(r_s)
        xp = xp + mod_p[:, 2][:, None, :] * op
        xs = xs + mod_s[:, 2][:, None, :] * osm
        hp = modulate(rmsnorm(xp, g_ffn[l]), mod_p[:, 3], mod_p[:, 4])
        hs = modulate(rmsnorm(xs, g_ffn[l]), mod_s[:, 3], mod_s[:, 4])
        zero_prev = jnp.zeros((hp.shape[0], CONV_W - 1, D_FF), hp.dtype)
        fp, conv_p = conv_ffn(hp, zero_prev, f_w_up[l], f_conv_w[l], f_conv_b[l], f_w_down[l])
        fs, conv_s = conv_ffn(hs, state_ffn_conv[l], f_w_up[l], f_conv_w[l], f_conv_b[l], f_w_down[l])
        cv_p.append(conv_p)
        cv_s.append(conv_s)
        xp = xp + mod_p[:, 5][:, None, :] * fp
        xs = xs + mod_s[:, 5][:, None, :] * fs
    y_prompt = rmsnorm(xp, g_final)
    y_sample = rmsnorm(xs, g_final)
    sdt = state_C_b.dtype
    return (y_prompt, y_sample,
            jnp.stack(ka_p), jnp.stack(ka_s), jnp.stack(va_p), jnp.stack(va_s),
            jnp.stack(cb_p).astype(sdt), jnp.stack(cb_s).astype(sdt),
            jnp.stack(nb_p).astype(sdt), jnp.stack(nb_s).astype(sdt),
            jnp.stack(mb_p).astype(sdt), jnp.stack(mb_s).astype(sdt),
            jnp.stack(ckv_p), jnp.stack(ckv_s), jnp.stack(kpe_p), jnp.stack(kpe_s),
            jnp.stack(cv_p), jnp.stack(cv_s))
```

```python
import functools
import math

import jax
import jax.numpy as jnp
from jax import lax
from jax.experimental import pallas as pl
from jax.experimental.pallas import tpu as pltpu

F32 = jnp.float32
BF16 = jnp.bfloat16

EPS = 1e-6
NEG_INF = -1e30
GATE_CAP = 15.0
ROPE_THETA = 10000.0
N_MIXERS = 3
CONV_W = 3
MLSTM_CHUNK = 64

V7X_VMEM_LIMIT_BYTES = 56 << 20
SUBLANES = 8
BF16_ROWS = 16

ROW_TILE = 1024
NORM_ROW_TILE = 512
COL_TILE = 512
FFN_COL_TILE = 256
FFN_PAD = 1024
FFN_DOWN_K_STEPS = 4
DIFF_TQ = 256
DIFF_TK = 256
MLA_TQ = 512
MLA_TK = 512
MLA_HEAD_BLOCK = 4
PAGES_PER_STEP = 8
MLA_PAGES_PER_STEP = 4


def _tile(n, pref, align):
    if n <= pref:
        return n
    t = (pref // align) * align
    while t >= align:
        if n % t == 0:
            return t
        t -= align
    raise ValueError(f"no tile for {n} (pref {pref}, align {align})")


def _params(n_axes):
    return pltpu.CompilerParams(dimension_semantics=("arbitrary",) * n_axes,
                                vmem_limit_bytes=V7X_VMEM_LIMIT_BYTES)


def _dot(a, b):
    return jnp.dot(a, b, preferred_element_type=F32)


def _dot_nt(a, b):
    return lax.dot_general(a, b, (((1,), (1,)), ((), ())), preferred_element_type=F32)


def _dot_tn(a, b):
    return lax.dot_general(a, b, (((0,), (0,)), ((), ())), preferred_element_type=F32)


def _row_tile(m, t, pref):
    if t >= BF16_ROWS:
        return _tile(t, pref, BF16_ROWS)
    return _tile(m, pref, BF16_ROWS)


def _vec_spec(vec, t, tm, tn, col_blk):
    if vec.ndim == 3:
        assert t % tm == 0
        return pl.BlockSpec((1, 1, tn), lambda i, *g: (i // (t // tm), 0, col_blk(*g)))
    assert tm % t == 0
    return pl.BlockSpec((tm // t, tn), lambda i, *g: (i, col_blk(*g)))


def _vec(ref):
    return ref[0] if len(ref.shape) == 3 else ref[...]


def _seq_apply(x, v, fn):
    g = v.shape[0]
    if g == 1:
        return fn(x, v)
    tm, n = x.shape
    return fn(x.reshape(g, tm // g, n), v[:, None, :]).reshape(tm, n)


def _pos_mul(x, tab):
    if tab.shape[0] == x.shape[0]:
        return x * tab
    tm, n = x.shape
    return (x.reshape(tm // tab.shape[0], tab.shape[0], n) * tab[None]).reshape(tm, n)


def _pos_spec(tab, t, tm, tn):
    if t >= tm:
        return pl.BlockSpec((tm, tn), lambda i, j, *g: (i % (t // tm), j))
    return pl.BlockSpec((t, tn), lambda i, j, *g: (0, j))


def _rms(x):
    return x * lax.rsqrt(jnp.mean(x * x, axis=-1, keepdims=True) + EPS)


def _ada_kernel(c_ref, w_ref, b_ref, tab_ref, os_ref, op_ref, *, depth, n_sample):
    c = c_ref[...]
    a = (c * jax.nn.sigmoid(c)).astype(BF16)
    base = _dot(a, w_ref[...].astype(BF16)) + b_ref[...]
    for l in range(depth):
        full = base + tab_ref[l]
        os_ref[l] = full[:n_sample]
        op_ref[l] = full[n_sample:]


def _ada(c_all, w_ada, b_ada, ada_table, n_sample):
    r, d = c_all.shape
    n = w_ada.shape[1]
    depth = ada_table.shape[0]
    tn = _tile(n, COL_TILE, 128)
    return pl.pallas_call(
        functools.partial(_ada_kernel, depth=depth, n_sample=n_sample),
        grid=(n // tn,),
        in_specs=[pl.BlockSpec((r, d), lambda j: (0, 0)),
                  pl.BlockSpec((d, tn), lambda j: (0, j)),
                  pl.BlockSpec((1, tn), lambda j: (0, j)),
                  pl.BlockSpec((depth, 1, tn), lambda j: (0, 0, j))],
        out_specs=[pl.BlockSpec((depth, n_sample, tn), lambda j: (0, 0, j)),
                   pl.BlockSpec((depth, r - n_sample, tn), lambda j: (0, 0, j))],
        out_shape=[jax.ShapeDtypeStruct((depth, n_sample, n), F32),
                   jax.ShapeDtypeStruct((depth, r - n_sample, n), F32)],
        compiler_params=_params(1), name="ada",
    )(c_all, w_ada, b_ada.reshape(1, n), ada_table.reshape(depth, 1, n))


def _norm_kernel(*refs, modulated):
    if modulated:
        x_ref, g_ref, sh_ref, sc_ref, o_ref = refs
    else:
        x_ref, g_ref, o_ref = refs
    y = _rms(x_ref[...]) * g_ref[...]
    if modulated:
        y = _seq_apply(y, _vec(sc_ref), lambda a, s: a * (1.0 + s))
        y = _seq_apply(y, _vec(sh_ref), lambda a, s: a + s)
    o_ref[...] = y.astype(o_ref.dtype)


def _norm(x, g, t, mod=None, shift_chunk=0, scale_chunk=1, out_dtype=BF16):
    m, d = x.shape
    tm = _row_tile(m, t, NORM_ROW_TILE)
    in_specs = [pl.BlockSpec((tm, d), lambda i: (i, 0)), pl.BlockSpec((1, d), lambda i: (0, 0))]
    args = [x, g.reshape(1, d)]
    if mod is not None:
        in_specs += [_vec_spec(mod, t, tm, d, lambda: shift_chunk),
                     _vec_spec(mod, t, tm, d, lambda: scale_chunk)]
        args += [mod, mod]
    return pl.pallas_call(
        functools.partial(_norm_kernel, modulated=mod is not None),
        grid=(m // tm,), in_specs=in_specs,
        out_specs=pl.BlockSpec((tm, d), lambda i: (i, 0)),
        out_shape=jax.ShapeDtypeStruct((m, d), out_dtype),
        compiler_params=_params(1), name="norm",
    )(*args)


def _linear_kernel(*refs, nk, residual):
    if residual:
        a_ref, w_ref, x_ref, gate_ref, o_ref, *scratch = refs
    else:
        a_ref, w_ref, o_ref, *scratch = refs
    part = _dot(a_ref[...].astype(BF16), w_ref[...].astype(BF16))

    def finish(acc):
        if residual:
            acc = x_ref[...] + _seq_apply(acc, _vec(gate_ref), lambda a, s: s * a)
        o_ref[...] = acc.astype(o_ref.dtype)

    if nk == 1:
        finish(part)
        return
    (acc_ref,) = scratch
    k = pl.program_id(2)

    @pl.when(k == 0)
    def _():
        acc_ref[...] = part

    @pl.when(k > 0)
    def _():
        acc_ref[...] += part

    @pl.when(k == nk - 1)
    def _():
        finish(acc_ref[...])


def _linear(a, w, t, *, n_out=None, col0=0, tn=COL_TILE, k_steps=1, out_dtype=F32, residual=None):
    m, kdim = a.shape
    n_out = n_out or w.shape[1]
    tm = _row_tile(m, t, ROW_TILE)
    tn = _tile(n_out, tn, 128)
    assert col0 % tn == 0 and kdim % k_steps == 0 and w.shape[0] == kdim
    tk = kdim // k_steps
    cb = col0 // tn
    in_specs = [pl.BlockSpec((tm, tk), lambda i, j, k: (i, k)),
                pl.BlockSpec((tk, tn), lambda i, j, k: (k, cb + j))]
    args = [a, w]
    if residual is not None:
        x, gate, chunk = residual
        assert x.shape == (m, n_out)
        in_specs += [pl.BlockSpec((tm, tn), lambda i, j, k: (i, j)),
                     _vec_spec(gate, t, tm, tn, lambda j, k: chunk * (n_out // tn) + j)]
        args += [x, gate]
    return pl.pallas_call(
        functools.partial(_linear_kernel, nk=k_steps, residual=residual is not None),
        grid=(m // tm, n_out // tn, k_steps), in_specs=in_specs,
        out_specs=pl.BlockSpec((tm, tn), lambda i, j, k: (i, j)),
        out_shape=jax.ShapeDtypeStruct((m, n_out), out_dtype),
        scratch_shapes=[pltpu.VMEM((tm, tn), F32)] if k_steps > 1 else [],
        compiler_params=_params(3), name="linear",
    )(*args)


def _ffn_up_kernel(*refs, t, n_valid, has_prev):
    if has_prev:
        h_ref, wa_ref, wu_ref, cw_ref, cb_ref, e1_ref, e2_ref, g_ref, tail_ref = refs
    else:
        h_ref, wa_ref, wu_ref, cw_ref, cb_ref, g_ref, tail_ref = refs
    j = pl.program_id(1)
    h = h_ref[...]
    a = _dot(h, wa_ref[...].astype(BF16))
    u = _dot(h, wu_ref[...].astype(BF16))
    tm = a.shape[0]
    r = lax.broadcasted_iota(jnp.int32, a.shape, 0)
    if tm != t:
        r = lax.rem(r, t)
    back1 = pltpu.roll(a, 1, 0)
    back2 = pltpu.roll(a, 2, 0)
    back1 = jnp.where(r >= 1, back1, e1_ref[...] if has_prev else 0.0)
    back2 = jnp.where(r >= 2, back2, e2_ref[...] if has_prev else 0.0)
    cw = cw_ref[...]
    ac = cb_ref[...] + cw[0:1] * back2 + cw[1:2] * back1 + cw[2:3] * a
    g = ac * jax.nn.sigmoid(ac) * u
    g_ref[...] = jnp.where(j < n_valid, g, 0.0).astype(g_ref.dtype)
    tail_ref[...] = a[tm - SUBLANES:, :] if t == tm else a


def _ffn_up(h, w_up, conv_w, conv_b, t, prev=None):
    m, d = h.shape
    f = conv_w.shape[1]
    tm = t if t >= BF16_ROWS else m
    assert t == tm or t == SUBLANES
    tn = _tile(f, FFN_COL_TILE, 128)
    n_valid = f // tn
    f_pad = -(-f // FFN_PAD) * FFN_PAD
    n_tiles = f_pad // tn
    tail_rows = SUBLANES if t == tm else tm

    def col(j):
        return jnp.minimum(j, n_valid - 1)

    in_specs = [pl.BlockSpec((tm, d), lambda i, j: (i, 0), pipeline_mode=pl.Buffered(1)),
                pl.BlockSpec((d, tn), lambda i, j: (0, col(j))),
                pl.BlockSpec((d, tn), lambda i, j: (0, n_valid + col(j))),
                pl.BlockSpec((CONV_W, tn), lambda i, j: (0, col(j))),
                pl.BlockSpec((1, tn), lambda i, j: (0, col(j)))]
    args = [h, w_up, w_up, conv_w, conv_b.reshape(1, f)]
    if prev is not None:
        n_seq = m // t
        zeros = jnp.zeros((n_seq, t - 2, f), F32)
        e2 = jnp.concatenate([prev, zeros], axis=1).reshape(m, f)
        e1 = jnp.concatenate([prev[:, 1:2], jnp.zeros((n_seq, 1, f), F32), zeros], axis=1).reshape(m, f)
        in_specs += [pl.BlockSpec((tm, tn), lambda i, j: (i, col(j)))] * 2
        args += [e1, e2]
    return pl.pallas_call(
        functools.partial(_ffn_up_kernel, t=t, n_valid=n_valid, has_prev=prev is not None),
        grid=(m // tm, n_tiles), in_specs=in_specs,
        out_specs=[pl.BlockSpec((tm, tn), lambda i, j: (i, j)),
                   pl.BlockSpec((tail_rows, tn), lambda i, j: (i, j))],
        out_shape=[jax.ShapeDtypeStruct((m, f_pad), BF16),
                   jax.ShapeDtypeStruct((m // tm * tail_rows, f_pad), F32)],
        compiler_params=_params(2), name="ffn_up",
    )(*args)


def _softmax_init(m_ref, l_ref, acc_ref):
    m_ref[...] = jnp.full(m_ref.shape, NEG_INF, F32)
    l_ref[...] = jnp.zeros(l_ref.shape, F32)
    acc_ref[...] = jnp.zeros(acc_ref.shape, F32)


def _softmax_step(s, v, m_ref, l_ref, acc_ref, idx):
    m_old = m_ref[idx]
    m_new = jnp.maximum(m_old, jnp.max(s, axis=1, keepdims=True))
    p = jnp.exp(s - m_new)
    alpha = jnp.exp(m_old - m_new)
    l_ref[idx] = alpha * l_ref[idx] + jnp.sum(p, axis=1, keepdims=True)
    acc_ref[idx] = alpha * acc_ref[idx] + _dot(p.astype(BF16), v)
    m_ref[idx] = m_new


def _stack_heads(zq, n_heads, width, offset):
    return jnp.concatenate(
        [zq[:, h * 2 * width + offset: h * 2 * width + offset + width] for h in range(n_heads)], axis=0)


def _diff_finish(lam_ref, gsub_ref, o_ref, m_ref, l_ref, acc_ref, n_heads, lam_init):
    lp = lam_ref[...]
    lam = (jnp.exp(jnp.sum(lp[0:1] * lp[1:2], axis=1, keepdims=True))
           - jnp.exp(jnp.sum(lp[2:3] * lp[3:4], axis=1, keepdims=True)) + lam_init)
    o = acc_ref[0] / l_ref[0] - lam * (acc_ref[1] / l_ref[1])
    y = _rms(o) * gsub_ref[...] * (1.0 - lam_init)
    tq = y.shape[0] // n_heads
    o_ref[...] = jnp.concatenate([y[h * tq:(h + 1) * tq] for h in range(n_heads)], axis=1).astype(o_ref.dtype)


def _diff_prompt_kernel(lam_ref, gsub_ref, zq_ref, k_ref, v_ref, o_ref, m_ref, l_ref, acc_ref,
                        *, n_heads, dh, lam_init):
    qi, kj = pl.program_id(1), pl.program_id(2)
    tq, tk = zq_ref.shape[0], k_ref.shape[0]

    @pl.when(kj == 0)
    def _():
        _softmax_init(m_ref, l_ref, acc_ref)

    @pl.when(kj <= qi)
    def _():
        zq = zq_ref[...]
        k = k_ref[...].astype(BF16)
        v = v_ref[...].astype(BF16)
        rows = n_heads * tq
        qpos = qi * tq + lax.rem(lax.broadcasted_iota(jnp.int32, (rows, tk), 0), tq)
        kpos = kj * tk + lax.broadcasted_iota(jnp.int32, (rows, tk), 1)
        visible = kpos <= qpos
        for c in range(2):
            q = _stack_heads(zq, n_heads, dh, c * dh).astype(BF16)
            s = _dot_nt(q, k[:, c * dh:(c + 1) * dh]) * (dh ** -0.5)
            _softmax_step(jnp.where(visible, s, NEG_INF), v, m_ref, l_ref, acc_ref, c)

    @pl.when(kj == qi)
    def _():
        _diff_finish(lam_ref, gsub_ref, o_ref, m_ref, l_ref, acc_ref, n_heads, lam_init)


def _diff_prompt(z, lam_p, g_sub, n_seq, t, n_heads, dh, lam_init):
    m = z.shape[0]
    qw = n_heads * 2 * dh
    tq = _tile(t, DIFF_TQ, BF16_ROWS)
    tk = tq
    nq = t // tq
    kcol = qw // (2 * dh)
    rows = n_heads * tq
    return pl.pallas_call(
        functools.partial(_diff_prompt_kernel, n_heads=n_heads, dh=dh, lam_init=lam_init),
        grid=(n_seq, nq, nq),
        in_specs=[pl.BlockSpec((4, dh), lambda b, qi, kj: (0, 0)),
                  pl.BlockSpec((1, 2 * dh), lambda b, qi, kj: (0, 0)),
                  pl.BlockSpec((tq, qw), lambda b, qi, kj: (b * nq + qi, 0)),
                  pl.BlockSpec((tk, 2 * dh), lambda b, qi, kj: (b * nq + jnp.minimum(kj, qi), kcol)),
                  pl.BlockSpec((tk, 2 * dh), lambda b, qi, kj: (b * nq + jnp.minimum(kj, qi), kcol + 1))],
        out_specs=pl.BlockSpec((tq, qw), lambda b, qi, kj: (b * nq + qi, 0)),
        out_shape=jax.ShapeDtypeStruct((m, qw), BF16),
        scratch_shapes=[pltpu.VMEM((2, rows, 1), F32), pltpu.VMEM((2, rows, 1), F32),
                        pltpu.VMEM((2, rows, 2 * dh), F32)],
        compiler_params=_params(3), name="diff_attn_prompt",
    )(lam_p, g_sub.reshape(1, 2 * dh), z, z, z)


def _pad_keys(x, n):
    return jnp.concatenate([x, jnp.zeros((n - x.shape[0], x.shape[1]), x.dtype)], axis=0)


def _new_key_mask(rows, t, n_keys):
    tok = lax.rem(lax.broadcasted_iota(jnp.int32, (rows, n_keys), 0), t)
    key = lax.broadcasted_iota(jnp.int32, (rows, n_keys), 1)
    return key <= tok


def _diff_sample_kernel(pt_ref, lam_ref, gsub_ref, zq_ref, kn_ref, vn_ref, *rest,
                        n_heads, dh, lam_init, pages, n_steps):
    del pt_ref
    kp_refs, vp_refs = rest[:pages], rest[pages:2 * pages]
    o_ref, m_ref, l_ref, acc_ref = rest[2 * pages:]
    g = pl.program_id(1)
    t = zq_ref.shape[0]
    zq = zq_ref[...]
    qs = [_stack_heads(zq, n_heads, dh, c * dh).astype(BF16) for c in range(2)]

    @pl.when(g == 0)
    def _():
        _softmax_init(m_ref, l_ref, acc_ref)

    k = jnp.concatenate([r[...] for r in kp_refs], axis=0).astype(BF16)
    v = jnp.concatenate([r[...] for r in vp_refs], axis=0).astype(BF16)
    for c in range(2):
        s = _dot_nt(qs[c], k[:, c * dh:(c + 1) * dh]) * (dh ** -0.5)
        _softmax_step(s, v, m_ref, l_ref, acc_ref, c)

    @pl.when(g == n_steps - 1)
    def _():
        page = kp_refs[0].shape[0]
        kn = _pad_keys(kn_ref[...], page).astype(BF16)
        vn = _pad_keys(vn_ref[...], page).astype(BF16)
        visible = _new_key_mask(n_heads * t, t, page)
        for c in range(2):
            s = _dot_nt(qs[c], kn[:, c * dh:(c + 1) * dh]) * (dh ** -0.5)
            _softmax_step(jnp.where(visible, s, NEG_INF), vn, m_ref, l_ref, acc_ref, c)
        _diff_finish(lam_ref, gsub_ref, o_ref, m_ref, l_ref, acc_ref, n_heads, lam_init)


def _diff_sample(z, cache_k, cache_v, layer_slot, page_table, lam_p, g_sub, t, n_heads, dh, lam_init):
    m = z.shape[0]
    n_seq, n_pages = page_table.shape
    page = cache_k.shape[2]
    qw = n_heads * 2 * dh
    kcol = qw // (2 * dh)
    pages = _tile(n_pages, PAGES_PER_STEP, 1)
    n_steps = n_pages // pages
    rows = n_heads * t

    def page_spec(p):
        return pl.BlockSpec((None, None, page, 2 * dh),
                            lambda s, g, pt: (layer_slot, pt[s, g * pages + p], 0, 0))

    grid_spec = pltpu.PrefetchScalarGridSpec(
        num_scalar_prefetch=1, grid=(n_seq, n_steps),
        in_specs=[pl.BlockSpec((4, dh), lambda s, g, pt: (0, 0)),
                  pl.BlockSpec((1, 2 * dh), lambda s, g, pt: (0, 0)),
                  pl.BlockSpec((t, qw), lambda s, g, pt: (s, 0)),
                  pl.BlockSpec((t, 2 * dh), lambda s, g, pt: (s, kcol)),
                  pl.BlockSpec((t, 2 * dh), lambda s, g, pt: (s, kcol + 1))]
                 + [page_spec(p) for p in range(pages)] * 2,
        out_specs=pl.BlockSpec((t, qw), lambda s, g, pt: (s, 0)),
        scratch_shapes=[pltpu.VMEM((2, rows, 1), F32), pltpu.VMEM((2, rows, 1), F32),
                        pltpu.VMEM((2, rows, 2 * dh), F32)])
    return pl.pallas_call(
        functools.partial(_diff_sample_kernel, n_heads=n_heads, dh=dh, lam_init=lam_init,
                          pages=pages, n_steps=n_steps),
        grid_spec=grid_spec, out_shape=jax.ShapeDtypeStruct((m, qw), F32),
        compiler_params=_params(2), name="diff_attn_sample",
    )(page_table, lam_p, g_sub.reshape(1, 2 * dh), z, z, z,
      *([cache_k] * pages), *([cache_v] * pages))


def _softcap(x):
    return GATE_CAP * jnp.tanh(x / GATE_CAP)


def _mlstm_kernel(*refs, n_chunks, has_init, scale):
    if has_init:
        (gb_ref, q_ref, k_ref, v_ref, zo_ref, zi_ref, zf_ref, gout_ref, c0_ref, n0_ref, m0_ref,
         y_ref, co_ref, no_ref, mo_ref, c_scr, n_scr, m_scr) = refs
    else:
        (gb_ref, q_ref, k_ref, v_ref, zo_ref, zi_ref, zf_ref, gout_ref,
         y_ref, co_ref, no_ref, mo_ref, c_scr, n_scr, m_scr) = refs
    hd, c = pl.program_id(1), pl.program_id(2)

    @pl.when(c == 0)
    def _():
        if has_init:
            c_scr[...] = c0_ref[...]
            n_scr[...] = n0_ref[...]
            m_scr[...] = m0_ref[...]
        else:
            c_scr[...] = jnp.zeros(c_scr.shape, F32)
            n_scr[...] = jnp.zeros(n_scr.shape, F32)
            m_scr[...] = jnp.zeros(m_scr.shape, F32)

    q = q_ref[...]
    k = k_ref[...] * scale
    v = v_ref[...]
    ln = q.shape[0]
    mm = BF16 if ln >= BF16_ROWS else F32
    ig = _softcap(zi_ref[...] + gb_ref[0, hd])
    lf = jax.nn.log_sigmoid(_softcap(zf_ref[...] + gb_ref[1, hd]))
    row = lax.broadcasted_iota(jnp.int32, (ln, ln), 0)
    col = lax.broadcasted_iota(jnp.int32, (ln, ln), 1)
    causal = col <= row
    diag = col == row
    bcum_c = jnp.sum(jnp.where(causal, jnp.broadcast_to(lf, (ln, ln)), 0.0), axis=1, keepdims=True)
    bcum_r = jnp.sum(jnp.where(diag, jnp.broadcast_to(bcum_c, (ln, ln)), 0.0), axis=0, keepdims=True)
    ig_c = jnp.sum(jnp.where(diag, jnp.broadcast_to(ig, (ln, ln)), 0.0), axis=1, keepdims=True)
    m_prev = m_scr[...]
    log_w = jnp.where(causal, bcum_c - bcum_r + ig, -jnp.inf)
    log_inter = bcum_c + m_prev
    m_t = jnp.maximum(log_inter, jnp.max(log_w, axis=1, keepdims=True))
    qm, km, vm = q.astype(mm), k.astype(mm), v.astype(mm)
    s = _dot_nt(qm, km) * jnp.exp(log_w - m_t)
    a_inter = jnp.exp(log_inter - m_t)
    c_old = c_scr[...]
    n_old = n_scr[...]
    num = _dot(s.astype(mm), vm) + a_inter * _dot(qm, c_old.astype(mm))
    den = jnp.sum(s, axis=1, keepdims=True) + a_inter * jnp.sum(q * n_old, axis=1, keepdims=True)
    hcell = num / jnp.maximum(jnp.abs(den), jnp.exp(-m_t))
    m_new = m_t[ln - 1:ln, :]
    b_last = bcum_c[ln - 1:ln, :]
    decay = jnp.exp(b_last + m_prev - m_new)
    w_in = jnp.exp(b_last - bcum_c + ig_c - m_new)
    kw = w_in * k
    c_scr[...] = decay * c_old + _dot_tn(kw.astype(mm), vm)
    n_scr[...] = decay * n_old + jnp.sum(kw, axis=0, keepdims=True)
    m_scr[...] = m_new
    y = _rms(hcell) * gout_ref[...] * jax.nn.sigmoid(zo_ref[...])
    y_ref[...] = y.astype(y_ref.dtype)

    @pl.when(c == n_chunks - 1)
    def _():
        co_ref[...] = c_scr[...]
        no_ref[...] = n_scr[...]
        mo_ref[...] = m_scr[...]


def _mlstm(z, zg, gate_bias, g_out, n_seq, t, n_heads, dqk, dv, init=None, out_dtype=BF16):
    m = z.shape[0]
    ln = _tile(t, MLSTM_CHUNK, SUBLANES)
    nc = t // ln
    gates = zg[:, :2 * n_heads].T.reshape(2 * n_heads, m // ln, 1, ln)
    kq, kk = 0, n_heads
    kv, kzo = 2 * n_heads * dqk // dv, 2 * n_heads * dqk // dv + n_heads
    in_specs = [pl.BlockSpec(memory_space=pltpu.SMEM),
                pl.BlockSpec((ln, dqk), lambda b, h, c: (b * nc + c, kq + h)),
                pl.BlockSpec((ln, dqk), lambda b, h, c: (b * nc + c, kk + h)),
                pl.BlockSpec((ln, dv), lambda b, h, c: (b * nc + c, kv + h)),
                pl.BlockSpec((ln, dv), lambda b, h, c: (b * nc + c, kzo + h)),
                pl.BlockSpec((None, None, 1, ln), lambda b, h, c: (h, b * nc + c, 0, 0)),
                pl.BlockSpec((None, None, 1, ln), lambda b, h, c: (n_heads + h, b * nc + c, 0, 0)),
                pl.BlockSpec((None, 1, dv), lambda b, h, c: (h, 0, 0))]
    args = [gate_bias, z, z, z, z, gates, gates, g_out.reshape(n_heads, 1, dv)]
    state_specs = [pl.BlockSpec((None, None, dqk, dv), lambda b, h, c: (b, h, 0, 0)),
                   pl.BlockSpec((None, None, 1, dqk), lambda b, h, c: (b, h, 0, 0)),
                   pl.BlockSpec((None, None, 1, 1), lambda b, h, c: (b, h, 0, 0))]
    if init is not None:
        c0, n0, m0 = init
        in_specs += state_specs
        args += [c0, n0.reshape(n_seq, n_heads, 1, dqk), m0.reshape(n_seq, n_heads, 1, 1)]
    y, c_new, n_new, m_new = pl.pallas_call(
        functools.partial(_mlstm_kernel, n_chunks=nc, has_init=init is not None, scale=dqk ** -0.5),
        grid=(n_seq, n_heads, nc), in_specs=in_specs,
        out_specs=[pl.BlockSpec((ln, dv), lambda b, h, c: (b * nc + c, h))] + state_specs,
        out_shape=[jax.ShapeDtypeStruct((m, n_heads * dv), out_dtype),
                   jax.ShapeDtypeStruct((n_seq, n_heads, dqk, dv), F32),
                   jax.ShapeDtypeStruct((n_seq, n_heads, 1, dqk), F32),
                   jax.ShapeDtypeStruct((n_seq, n_heads, 1, 1), F32)],
        scratch_shapes=[pltpu.VMEM((dqk, dv), F32), pltpu.VMEM((1, dqk), F32), pltpu.VMEM((1, 1), F32)],
        compiler_params=_params(3), name="mlstm",
    )(*args)
    return y, c_new, n_new.reshape(n_seq, n_heads, dqk), m_new.reshape(n_seq, n_heads)


def _mla_post_kernel(z_ref, z2_ref, gq_ref, gkv_ref, cos_ref, sin_ref, cq_ref, ckv_ref, kpe_ref, *, q_lora):
    z = z_ref[...]
    cq_ref[...] = (_rms(z[:, :q_lora]) * gq_ref[...]).astype(cq_ref.dtype)
    ckv_ref[...] = _rms(z[:, q_lora:]) * gkv_ref[...]
    z2 = z2_ref[...]
    r = z2.shape[1] // 2
    kpe_ref[...] = _pos_mul(z2[:, :r], cos_ref[...]) + _pos_mul(z2[:, r:], sin_ref[...])


def _mla_post(z, z2, g_q, g_kv, cos, sin, t):
    m, zc = z.shape
    q_lora, kv_lora, rope = g_q.shape[0], g_kv.shape[0], cos.shape[1]
    tm = _row_tile(m, t, NORM_ROW_TILE)
    return pl.pallas_call(
        functools.partial(_mla_post_kernel, q_lora=q_lora),
        grid=(m // tm, 1),
        in_specs=[pl.BlockSpec((tm, zc), lambda i, j: (i, 0)),
                  pl.BlockSpec((tm, 2 * rope), lambda i, j: (i, 0)),
                  pl.BlockSpec((1, q_lora), lambda i, j: (0, 0)),
                  pl.BlockSpec((1, kv_lora), lambda i, j: (0, 0)),
                  _pos_spec(cos, t, tm, rope), _pos_spec(sin, t, tm, rope)],
        out_specs=[pl.BlockSpec((tm, q_lora), lambda i, j: (i, 0)),
                   pl.BlockSpec((tm, kv_lora), lambda i, j: (i, 0)),
                   pl.BlockSpec((tm, rope), lambda i, j: (i, 0))],
        out_shape=[jax.ShapeDtypeStruct((m, q_lora), BF16),
                   jax.ShapeDtypeStruct((m, kv_lora), F32),
                   jax.ShapeDtypeStruct((m, rope), F32)],
        compiler_params=_params(2), name="mla_post",
    )(z, z2, g_q.reshape(1, q_lora), g_kv.reshape(1, kv_lora), cos, sin)


def _rope_linear_kernel(a_ref, w_ref, wrot_ref, cos_ref, sin_ref, o_ref):
    a = a_ref[...]
    x = _dot(a, w_ref[...].astype(BF16))
    x_rot = _dot(a, wrot_ref[...].astype(BF16))
    o_ref[...] = _pos_mul(x, cos_ref[...]) + _pos_mul(x_rot, sin_ref[...])


def _rope_linear(a, w, w_rot, cos, sin, t):
    m, kdim = a.shape
    n = w.shape[1]
    tm = _row_tile(m, t, ROW_TILE)
    tn = _tile(n, COL_TILE, 128)
    return pl.pallas_call(
        _rope_linear_kernel, grid=(m // tm, n // tn),
        in_specs=[pl.BlockSpec((tm, kdim), lambda i, j: (i, 0)),
                  pl.BlockSpec((kdim, tn), lambda i, j: (0, j)),
                  pl.BlockSpec((kdim, tn), lambda i, j: (0, j)),
                  _pos_spec(cos, t, tm, tn), _pos_spec(sin, t, tm, tn)],
        out_specs=pl.BlockSpec((tm, tn), lambda i, j: (i, j)),
        out_shape=jax.ShapeDtypeStruct((m, n), F32),
        compiler_params=_params(2), name="rope_linear",
    )(a, w, w_rot, cos, sin)


def _mla_prompt_kernel(qn_ref, qp_ref, kv_ref, kpe_ref, o_ref, m_ref, l_ref, acc_ref,
                       *, hb, nope, rope, vd):
    qi, kj = pl.program_id(2), pl.program_id(3)
    tq, tk = qn_ref.shape[0], kv_ref.shape[0]

    @pl.when(kj == 0)
    def _():
        _softmax_init(m_ref, l_ref, acc_ref)

    @pl.when(kj <= qi)
    def _():
        qn, qp, kv = qn_ref[...], qp_ref[...], kv_ref[...]
        kpe = kpe_ref[...].astype(BF16)
        qpos = qi * tq + lax.broadcasted_iota(jnp.int32, (tq, tk), 0)
        kpos = kj * tk + lax.broadcasted_iota(jnp.int32, (tq, tk), 1)
        visible = kpos <= qpos
        for h in range(hb):
            k_h = kv[:, h * (nope + vd): h * (nope + vd) + nope]
            v_h = kv[:, h * (nope + vd) + nope: (h + 1) * (nope + vd)]
            s = (_dot_nt(qn[:, h * nope:(h + 1) * nope].astype(BF16), k_h)
                 + _dot_nt(qp[:, h * rope:(h + 1) * rope].astype(BF16), kpe)) * ((nope + rope) ** -0.5)
            _softmax_step(jnp.where(visible, s, NEG_INF), v_h, m_ref, l_ref, acc_ref, h)

    @pl.when(kj == qi)
    def _():
        o_ref[...] = jnp.concatenate([acc_ref[h] / l_ref[h] for h in range(hb)], axis=1).astype(o_ref.dtype)


def _mla_prompt(qn, qp, kv, kpe, n_seq, t, n_heads, nope, rope, vd):
    m = qn.shape[0]
    tq = _tile(t, MLA_TQ, BF16_ROWS)
    tk = tq
    nq = t // tq
    hb = _tile(n_heads, MLA_HEAD_BLOCK, 2)
    return pl.pallas_call(
        functools.partial(_mla_prompt_kernel, hb=hb, nope=nope, rope=rope, vd=vd),
        grid=(n_seq, n_heads // hb, nq, nq),
        in_specs=[pl.BlockSpec((tq, hb * nope), lambda b, g, qi, kj: (b * nq + qi, g)),
                  pl.BlockSpec((tq, hb * rope), lambda b, g, qi, kj: (b * nq + qi, g)),
                  pl.BlockSpec((tk, hb * (nope + vd)), lambda b, g, qi, kj: (b * nq + jnp.minimum(kj, qi), g)),
                  pl.BlockSpec((tk, rope), lambda b, g, qi, kj: (b * nq + jnp.minimum(kj, qi), 0))],
        out_specs=pl.BlockSpec((tq, hb * vd), lambda b, g, qi, kj: (b * nq + qi, g)),
        out_shape=jax.ShapeDtypeStruct((m, n_heads * vd), BF16),
        scratch_shapes=[pltpu.VMEM((hb, tq, 1), F32), pltpu.VMEM((hb, tq, 1), F32),
                        pltpu.VMEM((hb, tq, vd), F32)],
        compiler_params=_params(4), name="mla_attn_prompt",
    )(qn, qp, kv, kpe)


def _q_latent_kernel(qn_ref, wuk_ref, o_ref):
    x = _dot_nt(qn_ref[...].astype(BF16), wuk_ref[...].astype(BF16))
    o_ref[...] = x.reshape(o_ref.shape)


def _q_latent(qn, w_kvb2, n_seq, t, n_heads, nope, vd):
    m = qn.shape[0]
    kvl = w_kvb2.shape[0]
    assert nope == vd
    return pl.pallas_call(
        _q_latent_kernel, grid=(n_heads,),
        in_specs=[pl.BlockSpec((m, nope), lambda h: (0, h)),
                  pl.BlockSpec((kvl, nope), lambda h: (0, 2 * h))],
        out_specs=pl.BlockSpec((n_seq, None, t, kvl), lambda h: (0, h, 0, 0)),
        out_shape=jax.ShapeDtypeStruct((n_seq, n_heads, t, kvl), F32),
        compiler_params=_params(1), name="mla_q_latent",
    )(qn, w_kvb2)


def _v_up_kernel(ol_ref, wuv_ref, o_ref):
    n_seq, t, kvl = ol_ref.shape
    x = ol_ref[...].reshape(n_seq * t, kvl).astype(BF16)
    o_ref[...] = _dot(x, wuv_ref[...].astype(BF16))


def _v_up(o_lat, w_kvb2, nope, vd):
    n_seq, n_heads, t, kvl = o_lat.shape
    assert nope == vd
    return pl.pallas_call(
        _v_up_kernel, grid=(n_heads,),
        in_specs=[pl.BlockSpec((n_seq, None, t, kvl), lambda h: (0, h, 0, 0)),
                  pl.BlockSpec((kvl, vd), lambda h: (0, 2 * h + 1))],
        out_specs=pl.BlockSpec((n_seq * t, vd), lambda h: (0, h)),
        out_shape=jax.ShapeDtypeStruct((n_seq * t, n_heads * vd), F32),
        compiler_params=_params(1), name="mla_v_up",
    )(o_lat, w_kvb2)


def _mla_sample_kernel(pt_ref, ql_ref, qp_ref, cn_ref, rn_ref, *rest, n_heads, rope, pages, n_steps, scale):
    del pt_ref
    cp_refs, rp_refs = rest[:pages], rest[pages:2 * pages]
    o_ref, m_ref, l_ref, acc_ref = rest[2 * pages:]
    g = pl.program_id(1)
    _, t, kvl = ql_ref.shape
    rows = n_heads * t
    q_lat = ql_ref[...].reshape(rows, kvl).astype(BF16)
    qp = qp_ref[...]
    q_pe = jnp.concatenate([qp[:, h * rope:(h + 1) * rope] for h in range(n_heads)], axis=0).astype(BF16)

    @pl.when(g == 0)
    def _():
        _softmax_init(m_ref, l_ref, acc_ref)

    ckv = jnp.concatenate([r[...] for r in cp_refs], axis=0).astype(BF16)
    kpe = jnp.concatenate([r[...] for r in rp_refs], axis=0).astype(BF16)
    s = (_dot_nt(q_lat, ckv) + _dot_nt(q_pe, kpe)) * scale
    _softmax_step(s, ckv, m_ref, l_ref, acc_ref, 0)

    @pl.when(g == n_steps - 1)
    def _():
        page = cp_refs[0].shape[0]
        cn = _pad_keys(cn_ref[...], page).astype(BF16)
        rn = _pad_keys(rn_ref[...], page).astype(BF16)
        sn = (_dot_nt(q_lat, cn) + _dot_nt(q_pe, rn)) * scale
        _softmax_step(jnp.where(_new_key_mask(rows, t, page), sn, NEG_INF), cn, m_ref, l_ref, acc_ref, 0)
        o_ref[...] = (acc_ref[0] / l_ref[0]).reshape(o_ref.shape)


def _mla_sample(q_lat, qp, ckv_new, kpe_new, cache_ckv, cache_kpe, layer_slot, page_table, nope):
    n_seq, n_heads, t, kvl = q_lat.shape
    rope = kpe_new.shape[1]
    n_pages = page_table.shape[1]
    page = cache_ckv.shape[2]
    pages = _tile(n_pages, MLA_PAGES_PER_STEP, 1)
    n_steps = n_pages // pages
    rows = n_heads * t

    def page_spec(width, p):
        return pl.BlockSpec((None, None, page, width),
                            lambda s, g, pt: (layer_slot, pt[s, g * pages + p], 0, 0))

    grid_spec = pltpu.PrefetchScalarGridSpec(
        num_scalar_prefetch=1, grid=(n_seq, n_steps),
        in_specs=[pl.BlockSpec((None, n_heads, t, kvl), lambda s, g, pt: (s, 0, 0, 0)),
                  pl.BlockSpec((t, n_heads * rope), lambda s, g, pt: (s, 0)),
                  pl.BlockSpec((t, kvl), lambda s, g, pt: (s, 0)),
                  pl.BlockSpec((t, rope), lambda s, g, pt: (s, 0))]
                 + [page_spec(kvl, p) for p in range(pages)]
                 + [page_spec(rope, p) for p in range(pages)],
        out_specs=pl.BlockSpec((None, n_heads, t, kvl), lambda s, g, pt: (s, 0, 0, 0)),
        scratch_shapes=[pltpu.VMEM((1, rows, 1), F32), pltpu.VMEM((1, rows, 1), F32),
                        pltpu.VMEM((1, rows, kvl), F32)])
    return pl.pallas_call(
        functools.partial(_mla_sample_kernel, n_heads=n_heads, rope=rope, pages=pages, n_steps=n_steps,
                          scale=(nope + rope) ** -0.5),
        grid_spec=grid_spec, out_shape=jax.ShapeDtypeStruct((n_seq, n_heads, t, kvl), F32),
        compiler_params=_params(2), name="mla_attn_sample",
    )(page_table, q_lat, qp, ckv_new, kpe_new, *([cache_ckv] * pages), *([cache_kpe] * pages))


def _rotate_half_cols(w, rope):
    k, n = w.shape
    w3 = w.reshape(k, n // rope, rope)
    half = rope // 2
    return jnp.concatenate([-w3[..., half:], w3[..., :half]], axis=-1).reshape(k, n)


def _rope_tables(pos, rope, repeat):
    half = rope // 2
    freqs = ROPE_THETA ** (-jnp.arange(half, dtype=F32) / half)
    ang = pos.astype(F32)[:, None] * freqs[None, :]
    cos = jnp.tile(jnp.cos(ang), (1, 2 * repeat))
    sin = jnp.tile(jnp.sin(ang), (1, 2 * repeat))
    return cos, sin


def _diff_lambda_init(layer):
    return 0.8 - 0.6 * math.exp(-0.3 * layer)


def kernel(x_prompt, x_sample, cache_k_a, cache_v_a, state_C_b, state_n_b, state_m_b, cache_ckv_c, cache_kpe_c,
           state_ffn_conv, page_table, c_prompt, c_sample, w_ada, b_ada, ada_table, g_mix, g_ffn, g_final,
           a_w_qkv, a_lambda, a_g_sub, a_w_o, b_w_in, b_gate_bias, b_g_out, b_w_out,
           c_w_in, c_g_q, c_g_kv, c_w_qb, c_w_kvb, c_w_o, f_w_up, f_conv_w, f_conv_b, f_w_down):
    nb, tp, d = x_prompt.shape
    ns, ts, _ = x_sample.shape
    depth = g_mix.shape[0]
    page = cache_k_a.shape[2]
    past_len = page_table.shape[1] * page
    assert ts == SUBLANES and f_conv_w.shape[1] == CONV_W

    dh = a_lambda.shape[-1]
    a_heads = a_w_o.shape[1] // (2 * dh)
    b_heads, dqk, dv = state_C_b.shape[2:]
    kvl, rope, q_lora = cache_ckv_c.shape[-1], cache_kpe_c.shape[-1], c_g_q.shape[-1]
    c_heads = c_w_kvb.shape[2]
    nope = c_w_qb.shape[2] // c_heads - rope
    vd = c_w_kvb.shape[3] - nope
    d_ff = f_conv_w.shape[-1]

    groups = (("p", nb, tp), ("s", ns, ts))
    x = {"p": x_prompt.reshape(nb * tp, d), "s": x_sample.reshape(ns * ts, d)}

    pad = -nb % SUBLANES
    c_all = jnp.concatenate([c_sample, c_prompt, jnp.zeros((pad, d), F32)], axis=0)
    mod_s, mod_p = _ada(c_all, w_ada, b_ada, ada_table, ns)
    mods = {"p": mod_p[:, :nb].reshape(depth, nb, 1, -1), "s": mod_s}

    pos = {"p": jnp.arange(tp, dtype=jnp.int32), "s": past_len + jnp.arange(ts, dtype=jnp.int32)}
    cache_k = cache_k_a.reshape(cache_k_a.shape[:3] + (2 * dh,))
    cache_v = cache_v_a.reshape(cache_v_a.shape[:3] + (2 * dh,))

    outs = {key: {"p": [], "s": []} for key in ("ka", "va", "cb", "nb", "mb", "ckv", "kpe", "cv")}

    for l in range(depth):
        kind, slot = l % N_MIXERS, l // N_MIXERS
        w_down = jnp.pad(f_w_down[l].astype(BF16), ((0, -d_ff % FFN_PAD), (0, 0)))
        if kind == 1:
            n_main = 2 * b_heads * (dqk + dv)
            w_gate = jnp.pad(b_w_in[slot][:, n_main:], ((0, 0), (0, 128 - 2 * b_heads)))
        elif kind == 2:
            w_kpe = c_w_in[slot][:, q_lora + kvl:]
            w_kpe2 = jnp.concatenate([w_kpe, _rotate_half_cols(w_kpe, rope)], axis=1)
            wq3 = c_w_qb[slot].reshape(q_lora, c_heads, nope + rope)
            w_qn = wq3[:, :, :nope].reshape(q_lora, c_heads * nope)
            w_qr = wq3[:, :, nope:].reshape(q_lora, c_heads * rope)
            w_qr_rot = _rotate_half_cols(w_qr, rope)
            w_kvb2 = c_w_kvb[slot].reshape(kvl, c_heads * (nope + vd))

        for key, n_seq, t in groups:
            mod = mods[key][l]
            xg = x[key]
            h = _norm(xg, g_mix[l], t, mod, 0, 1)
            if kind == 0:
                lam_init = _diff_lambda_init(l)
                z = _linear(h, a_w_qkv[slot], t)
                qw = a_heads * 2 * dh
                k_new, v_new = z[:, qw:qw + 2 * dh], z[:, qw + 2 * dh:]
                outs["ka"][key].append(k_new.reshape(n_seq, t, 1, 2 * dh))
                outs["va"][key].append(v_new.reshape(n_seq, t, 1, 2 * dh))
                if key == "p":
                    o = _diff_prompt(z, a_lambda[slot], a_g_sub[slot], n_seq, t, a_heads, dh, lam_init)
                else:
                    o = _diff_sample(z, cache_k, cache_v, slot, page_table, a_lambda[slot], a_g_sub[slot],
                                     t, a_heads, dh, lam_init).astype(BF16)
                xg = _linear(o, a_w_o[slot], t, residual=(xg, mod, 2))
            elif kind == 1:
                z = _linear(h, b_w_in[slot], t, n_out=n_main)
                zg = _linear(h, w_gate, t)
                if key == "p":
                    y, c_new, n_new, m_new = _mlstm(z, zg, b_gate_bias[slot], b_g_out[slot], n_seq, t,
                                                    b_heads, dqk, dv)
                else:
                    y, c_new, n_new, m_new = _mlstm(z, zg, b_gate_bias[slot], b_g_out[slot], n_seq, t,
                                                    b_heads, dqk, dv, out_dtype=F32,
                                                    init=(state_C_b[slot], state_n_b[slot], state_m_b[slot]))
                    y = y.astype(BF16)
                outs["cb"][key].append(c_new)
                outs["nb"][key].append(n_new)
                outs["mb"][key].append(m_new)
                xg = _linear(y, b_w_out[slot], t, residual=(xg, mod, 2))
            else:
                z = _linear(h, c_w_in[slot], t, n_out=q_lora + kvl)
                z2 = _linear(h, w_kpe2, t)
                cos1, sin1 = _rope_tables(pos[key], rope, 1)
                cq, ckv, kpe = _mla_post(z, z2, c_g_q[slot], c_g_kv[slot], cos1, sin1, t)
                outs["ckv"][key].append(ckv.reshape(n_seq, t, kvl))
                outs["kpe"][key].append(kpe.reshape(n_seq, t, rope))
                cosh, sinh = _rope_tables(pos[key], rope, c_heads)
                qn = _linear(cq, w_qn, t)
                qp = _rope_linear(cq, w_qr, w_qr_rot, cosh, sinh, t)
                if key == "p":
                    kv = _linear(ckv, w_kvb2, t, out_dtype=BF16)
                    o = _mla_prompt(qn, qp, kv, kpe, n_seq, t, c_heads, nope, rope, vd)
                else:
                    q_lat = _q_latent(qn, w_kvb2, n_seq, t, c_heads, nope, vd)
                    o_lat = _mla_sample(q_lat, qp, ckv, kpe, cache_ckv_c, cache_kpe_c, slot, page_table, nope)
                    o = _v_up(o_lat, w_kvb2, nope, vd).astype(BF16)
                xg = _linear(o, c_w_o[slot], t, k_steps=2, residual=(xg, mod, 2))

            h = _norm(xg, g_ffn[l], t, mod, 3, 4)
            prev = state_ffn_conv[l] if key == "s" else None
            g, tail = _ffn_up(h, f_w_up[l], f_conv_w[l], f_conv_b[l], t, prev)
            outs["cv"][key].append(tail.reshape(n_seq, SUBLANES, -1)[:, SUBLANES - (CONV_W - 1):, :d_ff])
            x[key] = _linear(g, w_down, t, k_steps=FFN_DOWN_K_STEPS, residual=(xg, mod, 5))

    y_p = _norm(x["p"], g_final, tp, out_dtype=F32).reshape(nb, tp, d)
    y_s = _norm(x["s"], g_final, ts, out_dtype=F32).reshape(ns, ts, d)
    res = [y_p, y_s]
    for key in ("ka", "va", "cb", "nb", "mb", "ckv", "kpe", "cv"):
        res += [jnp.stack(outs[key]["p"]), jnp.stack(outs[key]["s"])]
    return tuple(res)
```

```python
import functools
import math

import jax
import jax.numpy as jnp
from jax import lax
from jax.experimental import pallas as pl
from jax.experimental.pallas import tpu as pltpu

F32 = jnp.float32
BF16 = jnp.bfloat16

EPS = 1e-6
NEG_INF = -1e30
GATE_CAP = 15.0
ROPE_THETA = 10000.0
N_MIXERS = 3
CONV_W = 3

V7X_VMEM_LIMIT_BYTES = 56 << 20
SUBLANES = 8
BF16_ROWS = 16

ROW_TILE = 1024
NORM_ROW_TILE = 512
COL_TILE = 512
DOWN_COL_TILE = 1024
FFN_COL_TILE = 256
FFN_PAD = 1024
FFN_DOWN_K_STEPS = 4
CAST_ROW_TILE = 256
MLSTM_CHUNK = 128
DIFF_TQ = 256
DIFF_TK = 1024
DIFF_HEADS_PER_ITER = 1
DIFF_UNROLL = 2
MLA_TQ = 256
MLA_TK = 1024
MLA_HEAD_BLOCK = 4
PAGES_PER_STEP = 16
MLA_PAGES_PER_STEP = 16
PAGES_PER_CHUNK = 16


def _tile(n, pref, align):
    if n <= pref:
        return n
    t = (pref // align) * align
    while t >= align:
        if n % t == 0:
            return t
        t -= align
    raise ValueError(f"no tile for {n} (pref {pref}, align {align})")


def _params(n_axes):
    return pltpu.CompilerParams(dimension_semantics=("arbitrary",) * n_axes,
                                vmem_limit_bytes=V7X_VMEM_LIMIT_BYTES)


def _dot(a, b):
    return jnp.dot(a, b, preferred_element_type=F32)


def _dot_nt(a, b):
    return lax.dot_general(a, b, (((1,), (1,)), ((), ())), preferred_element_type=F32)


def _dot_tn(a, b):
    return lax.dot_general(a, b, (((0,), (0,)), ((), ())), preferred_element_type=F32)


def _w_spec(w, layer, blk, idx):
    if w.ndim == len(blk) + 1:
        return pl.BlockSpec((None,) + tuple(blk), lambda *g: (layer,) + tuple(idx(*g)))
    return pl.BlockSpec(tuple(blk), idx)


def _row_tile(m, t, pref):
    if t >= BF16_ROWS:
        return _tile(t, pref, BF16_ROWS)
    return _tile(m, pref, BF16_ROWS)


def _vec_spec(vec, layer, t, tm, tn, col_blk):
    if vec.ndim == 4:
        assert t % tm == 0
        return pl.BlockSpec((None, 1, 1, tn), lambda i, *g: (layer, i // (t // tm), 0, col_blk(*g)))
    assert tm % t == 0
    return pl.BlockSpec((None, tm // t, tn), lambda i, *g: (layer, i, col_blk(*g)))


def _vec(ref):
    return ref[0] if len(ref.shape) == 3 else ref[...]


def _seq_apply(x, v, fn):
    g = v.shape[0]
    if g == 1:
        return fn(x, v)
    tm, n = x.shape
    return fn(x.reshape(g, tm // g, n), v[:, None, :]).reshape(tm, n)


def _pos_mul(x, tab):
    if tab.shape[0] == x.shape[0]:
        return x * tab
    tm, n = x.shape
    return (x.reshape(tm // tab.shape[0], tab.shape[0], n) * tab[None]).reshape(tm, n)


def _pos_spec(tab, t, tm, tn):
    if t >= tm:
        return pl.BlockSpec((tm, tn), lambda i, j, *g: (i % (t // tm), j))
    return pl.BlockSpec((t, tn), lambda i, j, *g: (0, j))


def _rms(x):
    return x * lax.rsqrt(jnp.mean(x * x, axis=-1, keepdims=True) + EPS)


def _ada_kernel(c_ref, w_ref, b_ref, tab_ref, os_ref, op_ref, *, depth, n_sample):
    c = c_ref[...]
    a = (c * jax.nn.sigmoid(c)).astype(BF16)
    base = _dot(a, w_ref[...].astype(BF16)) + b_ref[...]
    for l in range(depth):
        full = base + tab_ref[l]
        os_ref[l] = full[:n_sample]
        op_ref[l] = full[n_sample:]


def _ada(c_all, w_ada, b_ada, ada_table, n_sample):
    r, d = c_all.shape
    n = w_ada.shape[1]
    depth = ada_table.shape[0]
    tn = _tile(n, COL_TILE, 128)
    return pl.pallas_call(
        functools.partial(_ada_kernel, depth=depth, n_sample=n_sample),
        grid=(n // tn,),
        in_specs=[pl.BlockSpec((r, d), lambda j: (0, 0)),
                  pl.BlockSpec((d, tn), lambda j: (0, j)),
                  pl.BlockSpec((1, tn), lambda j: (0, j)),
                  pl.BlockSpec((depth, 1, tn), lambda j: (0, 0, j))],
        out_specs=[pl.BlockSpec((depth, n_sample, tn), lambda j: (0, 0, j)),
                   pl.BlockSpec((depth, r - n_sample, tn), lambda j: (0, 0, j))],
        out_shape=[jax.ShapeDtypeStruct((depth, n_sample, n), F32),
                   jax.ShapeDtypeStruct((depth, r - n_sample, n), F32)],
        compiler_params=_params(1), name="ada",
    )(c_all, w_ada, b_ada.reshape(1, n), ada_table.reshape(depth, 1, n))


def _norm_kernel(*refs, modulated):
    if modulated:
        x_ref, g_ref, sh_ref, sc_ref, o_ref = refs
    else:
        x_ref, g_ref, o_ref = refs
    y = _rms(x_ref[...]) * g_ref[...]
    if modulated:
        y = _seq_apply(y, _vec(sc_ref), lambda a, s: a * (1.0 + s))
        y = _seq_apply(y, _vec(sh_ref), lambda a, s: a + s)
    o_ref[...] = y.astype(o_ref.dtype)


def _norm(x, g, layer, t, mod=None, shift_chunk=0, scale_chunk=1, out_dtype=BF16):
    m, d = x.shape
    tm = _row_tile(m, t, NORM_ROW_TILE)
    g = g.reshape(g.shape[:-1] + (1, d))
    in_specs = [pl.BlockSpec((tm, d), lambda i: (i, 0)), _w_spec(g, layer, (1, d), lambda i: (0, 0))]
    args = [x, g]
    if mod is not None:
        in_specs += [_vec_spec(mod, layer, t, tm, d, lambda: shift_chunk),
                     _vec_spec(mod, layer, t, tm, d, lambda: scale_chunk)]
        args += [mod, mod]
    return pl.pallas_call(
        functools.partial(_norm_kernel, modulated=mod is not None),
        grid=(m // tm,), in_specs=in_specs,
        out_specs=pl.BlockSpec((tm, d), lambda i: (i, 0)),
        out_shape=jax.ShapeDtypeStruct((m, d), out_dtype),
        compiler_params=_params(1), name="norm",
    )(*args)


def _linear_kernel(*refs, nk, residual):
    if residual:
        a_ref, w_ref, x_ref, gate_ref, o_ref, *scratch = refs
    else:
        a_ref, w_ref, o_ref, *scratch = refs
    part = _dot(a_ref[...].astype(BF16), w_ref[...].astype(BF16))

    def finish(acc):
        if residual:
            acc = x_ref[...] + _seq_apply(acc, _vec(gate_ref), lambda a, s: s * a)
        o_ref[...] = acc.astype(o_ref.dtype)

    if nk == 1:
        finish(part)
        return
    (acc_ref,) = scratch
    k = pl.program_id(2)

    @pl.when(k == 0)
    def _():
        acc_ref[...] = part

    @pl.when(k > 0)
    def _():
        acc_ref[...] += part

    @pl.when(k == nk - 1)
    def _():
        finish(acc_ref[...])


def _linear(a, w, t, *, layer=0, n_out=None, col0=0, tn=COL_TILE, k_steps=1, out_dtype=F32, residual=None):
    m, kdim = a.shape
    n_out = n_out or w.shape[-1]
    tm = _row_tile(m, t, ROW_TILE)
    tn = _tile(n_out, tn, 128)
    assert col0 % tn == 0 and kdim % k_steps == 0 and w.shape[-2] == kdim
    tk = kdim // k_steps
    cb = col0 // tn
    in_specs = [pl.BlockSpec((tm, tk), lambda i, j, k: (i, k)),
                _w_spec(w, layer, (tk, tn), lambda i, j, k: (k, cb + j))]
    args = [a, w]
    if residual is not None:
        x, gate, mod_layer, chunk = residual
        assert x.shape == (m, n_out)
        in_specs += [pl.BlockSpec((tm, tn), lambda i, j, k: (i, j)),
                     _vec_spec(gate, mod_layer, t, tm, tn, lambda j, k: chunk * (n_out // tn) + j)]
        args += [x, gate]
    return pl.pallas_call(
        functools.partial(_linear_kernel, nk=k_steps, residual=residual is not None),
        grid=(m // tm, n_out // tn, k_steps), in_specs=in_specs,
        out_specs=pl.BlockSpec((tm, tn), lambda i, j, k: (i, j)),
        out_shape=jax.ShapeDtypeStruct((m, n_out), out_dtype),
        scratch_shapes=[pltpu.VMEM((tm, tn), F32)] if k_steps > 1 else [],
        compiler_params=_params(3), name="linear",
    )(*args)


def _cast_pad_kernel(w_ref, o_ref, *, n_valid):
    i = pl.program_id(0)
    o_ref[...] = jnp.where(i < n_valid, w_ref[...], 0.0).astype(o_ref.dtype)


def _cast_pad_rows(w, layer, rows_pad):
    _, k, n = w.shape
    tr = _tile(math.gcd(k, rows_pad), CAST_ROW_TILE, BF16_ROWS)
    n_valid = k // tr
    return pl.pallas_call(
        functools.partial(_cast_pad_kernel, n_valid=n_valid),
        grid=(rows_pad // tr,),
        in_specs=[pl.BlockSpec((None, tr, n), lambda i: (layer, jnp.minimum(i, n_valid - 1), 0))],
        out_specs=pl.BlockSpec((tr, n), lambda i: (i, 0)),
        out_shape=jax.ShapeDtypeStruct((rows_pad, n), BF16),
        compiler_params=_params(1), name="cast_pad",
    )(w)


def _ffn_up_kernel(*refs, t, n_valid, has_prev):
    if has_prev:
        h_ref, wa_ref, wu_ref, cw_ref, cb_ref, e1_ref, e2_ref, g_ref, tail_ref = refs
    else:
        h_ref, wa_ref, wu_ref, cw_ref, cb_ref, g_ref, tail_ref = refs
    j = pl.program_id(1)
    h = h_ref[...]
    a = _dot(h, wa_ref[...].astype(BF16))
    u = _dot(h, wu_ref[...].astype(BF16))
    tm = a.shape[0]
    r = lax.broadcasted_iota(jnp.int32, a.shape, 0)
    if tm != t:
        r = lax.rem(r, t)
    back1 = pltpu.roll(a, 1, 0)
    back2 = pltpu.roll(a, 2, 0)
    back1 = jnp.where(r >= 1, back1, e1_ref[...] if has_prev else 0.0)
    back2 = jnp.where(r >= 2, back2, e2_ref[...] if has_prev else 0.0)
    cw = cw_ref[...]
    ac = cb_ref[...] + cw[0:1] * back2 + cw[1:2] * back1 + cw[2:3] * a
    g = ac * jax.nn.sigmoid(ac) * u
    g_ref[...] = jnp.where(j < n_valid, g, 0.0).astype(g_ref.dtype)
    tail_ref[...] = a[tm - SUBLANES:, :] if t == tm else a


def _ffn_up(h, w_up, conv_w, conv_b, layer, t, prev=None):
    m, d = h.shape
    f = conv_w.shape[-1]
    tm = t if t >= BF16_ROWS else m
    assert t == tm or t == SUBLANES
    tn = _tile(f, FFN_COL_TILE, 128)
    n_valid = f // tn
    f_pad = -(-f // FFN_PAD) * FFN_PAD
    n_tiles = f_pad // tn
    tail_rows = SUBLANES if t == tm else tm

    def col(j):
        return jnp.minimum(j, n_valid - 1)

    conv_b = conv_b.reshape(conv_b.shape[0], 1, f)
    in_specs = [pl.BlockSpec((tm, d), lambda i, j: (i, 0), pipeline_mode=pl.Buffered(1)),
                _w_spec(w_up, layer, (d, tn), lambda i, j: (0, col(j))),
                _w_spec(w_up, layer, (d, tn), lambda i, j: (0, n_valid + col(j))),
                _w_spec(conv_w, layer, (CONV_W, tn), lambda i, j: (0, col(j))),
                _w_spec(conv_b, layer, (1, tn), lambda i, j: (0, col(j)))]
    args = [h, w_up, w_up, conv_w, conv_b]
    if prev is not None:
        n_seq = m // t
        zeros = jnp.zeros((n_seq, t - 2, f), F32)
        e2 = jnp.concatenate([prev, zeros], axis=1).reshape(m, f)
        e1 = jnp.concatenate([prev[:, 1:2], jnp.zeros((n_seq, 1, f), F32), zeros], axis=1).reshape(m, f)
        in_specs += [pl.BlockSpec((tm, tn), lambda i, j: (i, col(j)))] * 2
        args += [e1, e2]
    return pl.pallas_call(
        functools.partial(_ffn_up_kernel, t=t, n_valid=n_valid, has_prev=prev is not None),
        grid=(m // tm, n_tiles), in_specs=in_specs,
        out_specs=[pl.BlockSpec((tm, tn), lambda i, j: (i, j)),
                   pl.BlockSpec((tail_rows, tn), lambda i, j: (i, j))],
        out_shape=[jax.ShapeDtypeStruct((m, f_pad), BF16),
                   jax.ShapeDtypeStruct((m // tm * tail_rows, f_pad), F32)],
        compiler_params=_params(2), name="ffn_up",
    )(*args)


def _softmax_fold(s, v, m_old, l_old, acc_old):
    m_new = jnp.maximum(m_old, jnp.max(s, axis=1, keepdims=True))
    p = jnp.exp(s - m_new)
    alpha = jnp.exp(m_old - m_new)
    l_new = alpha * l_old + jnp.sum(p, axis=1, keepdims=True)
    acc_new = alpha * acc_old + _dot(p.astype(BF16), v)
    return m_new, l_new, acc_new


def _softmax_init(m_ref, l_ref, acc_ref):
    m_ref[...] = jnp.full(m_ref.shape, NEG_INF, F32)
    l_ref[...] = jnp.zeros(l_ref.shape, F32)
    acc_ref[...] = jnp.zeros(acc_ref.shape, F32)


def _causal_pairs(nq, tq, tk):
    qi, kj, last = [], [], []
    for q in range(nq):
        n_kv = -(-(q + 1) * tq // tk)
        for k in range(n_kv):
            qi.append(q)
            kj.append(k)
            last.append(int(k == n_kv - 1))
    return tuple(jnp.asarray(a, jnp.int32) for a in (qi, kj, last))


def _causal_visible(qi, kj, tq, tk, rows):
    row = lax.rem(lax.broadcasted_iota(jnp.int32, (rows, tk), 0), tq)
    col = lax.broadcasted_iota(jnp.int32, (rows, tk), 1)
    return kj * tk + col <= qi * tq + row


def _stack_heads(zq, n_heads, width, offset):
    return jnp.concatenate(
        [zq[:, h * 2 * width + offset: h * 2 * width + offset + width] for h in range(n_heads)], axis=0)


def _diff_lambda(lam_ref, lam_init):
    lp = lam_ref[...]
    return (jnp.exp(jnp.sum(lp[0:1] * lp[1:2], axis=1, keepdims=True))
            - jnp.exp(jnp.sum(lp[2:3] * lp[3:4], axis=1, keepdims=True)) + lam_init)


def _diff_finish(lam_ref, gsub_ref, o_ref, m_ref, l_ref, acc_ref, n_heads, lam_init):
    lam = _diff_lambda(lam_ref, lam_init)
    tq = acc_ref.shape[1] // n_heads
    w = acc_ref.shape[2]
    for h in range(n_heads):
        rows = slice(h * tq, (h + 1) * tq)
        o = acc_ref[0, rows, :] / l_ref[0, rows, :] - lam * (acc_ref[1, rows, :] / l_ref[1, rows, :])
        y = _rms(o) * gsub_ref[...] * (1.0 - lam_init)
        o_ref[:, h * w:(h + 1) * w] = y.astype(o_ref.dtype)


def _diff_prompt_kernel(qi_ref, kj_ref, last_ref, lam_ref, gsub_ref, zq_ref, k_ref, v_ref, o_ref,
                        m_ref, l_ref, acc_ref, q_scr, k_scr, v_scr, *, n_heads, dh, lam_init, hpi, unroll):
    pair = pl.program_id(1)
    qi, kj, last = qi_ref[pair], kj_ref[pair], last_ref[pair]
    tq, tk = zq_ref.shape[0], k_ref.shape[0]

    @pl.when(kj == 0)
    def _():
        _softmax_init(m_ref, l_ref, acc_ref)
        zq = zq_ref[...]
        for c in range(2):
            q_scr[c] = _stack_heads(zq, n_heads, dh, c * dh).astype(BF16)

    k_scr[...] = k_ref[...].astype(BF16)
    v_scr[...] = v_ref[...].astype(BF16)

    def run(masked):
        rc = hpi * tq
        if masked:
            visible = _causal_visible(qi, kj, tq, tk, rc)

        def body(i, carry):
            rows = pl.ds(pl.multiple_of(i * rc, rc), rc)
            v = v_scr[...]
            for c in range(2):
                s = _dot_nt(q_scr[c, rows, :], k_scr[:, c * dh:(c + 1) * dh]) * (dh ** -0.5)
                if masked:
                    s = jnp.where(visible, s, NEG_INF)
                m_new, l_new, acc_new = _softmax_fold(s, v, m_ref[c, rows, :], l_ref[c, rows, :],
                                                      acc_ref[c, rows, :])
                m_ref[c, rows, :] = m_new
                l_ref[c, rows, :] = l_new
                acc_ref[c, rows, :] = acc_new
            return carry

        lax.fori_loop(0, n_heads // hpi, body, 0, unroll=unroll)

    @pl.when(last == 0)
    def _():
        run(False)

    @pl.when(last == 1)
    def _():
        run(True)
        _diff_finish(lam_ref, gsub_ref, o_ref, m_ref, l_ref, acc_ref, n_heads, lam_init)


def _diff_prompt(z, lam_p, g_sub, n_seq, t, n_heads, dh, lam_init):
    m = z.shape[0]
    qw = n_heads * 2 * dh
    tq = _tile(t, DIFF_TQ, BF16_ROWS)
    tk = _tile(t, DIFF_TK, tq)
    nq, nk = t // tq, t // tk
    kcol = qw // (2 * dh)
    rows = n_heads * tq
    tabs = _causal_pairs(nq, tq, tk)
    grid_spec = pltpu.PrefetchScalarGridSpec(
        num_scalar_prefetch=3, grid=(n_seq, tabs[0].shape[0]),
        in_specs=[pl.BlockSpec((4, dh), lambda b, p, qi, kj, la: (0, 0)),
                  pl.BlockSpec((1, 2 * dh), lambda b, p, qi, kj, la: (0, 0)),
                  pl.BlockSpec((tq, qw), lambda b, p, qi, kj, la: (b * nq + qi[p], 0)),
                  pl.BlockSpec((tk, 2 * dh), lambda b, p, qi, kj, la: (b * nk + kj[p], kcol)),
                  pl.BlockSpec((tk, 2 * dh), lambda b, p, qi, kj, la: (b * nk + kj[p], kcol + 1))],
        out_specs=pl.BlockSpec((tq, qw), lambda b, p, qi, kj, la: (b * nq + qi[p], 0)),
        scratch_shapes=[pltpu.VMEM((2, rows, 1), F32), pltpu.VMEM((2, rows, 1), F32),
                        pltpu.VMEM((2, rows, 2 * dh), F32), pltpu.VMEM((2, rows, dh), BF16),
                        pltpu.VMEM((tk, 2 * dh), BF16), pltpu.VMEM((tk, 2 * dh), BF16)])
    return pl.pallas_call(
        functools.partial(_diff_prompt_kernel, n_heads=n_heads, dh=dh, lam_init=lam_init,
                          hpi=_tile(n_heads, DIFF_HEADS_PER_ITER, 1), unroll=DIFF_UNROLL),
        grid_spec=grid_spec, out_shape=jax.ShapeDtypeStruct((m, qw), BF16),
        compiler_params=_params(2), name="diff_attn_prompt",
    )(*tabs, lam_p, g_sub.reshape(1, 2 * dh), z, z, z)


def _pad_keys(x, n):
    return jnp.concatenate([x, jnp.zeros((n - x.shape[0], x.shape[1]), x.dtype)], axis=0)


def _new_key_mask(rows, t, n_keys):
    tok = lax.rem(lax.broadcasted_iota(jnp.int32, (rows, n_keys), 0), t)
    key = lax.broadcasted_iota(jnp.int32, (rows, n_keys), 1)
    return key <= tok


def _diff_sample_kernel(pt_ref, lam_ref, gsub_ref, zq_ref, kn_ref, vn_ref, *rest,
                        n_heads, dh, lam_init, pages, chunk, n_steps):
    del pt_ref
    kp_refs, vp_refs = rest[:pages], rest[pages:2 * pages]
    o_ref, m_ref, l_ref, acc_ref, q_scr = rest[2 * pages:]
    g = pl.program_id(1)
    t = zq_ref.shape[0]
    page = kp_refs[0].shape[0] // 2
    scale = dh ** -0.5

    @pl.when(g == 0)
    def _():
        _softmax_init(m_ref, l_ref, acc_ref)
        zq = zq_ref[...]
        for c in range(2):
            q_scr[c] = _stack_heads(zq, n_heads, dh, c * dh).astype(BF16)

    def half(ref, c):
        return ref[pl.ds(c, page, stride=2), :]

    qs = [q_scr[0], q_scr[1]]
    state = [(m_ref[c], l_ref[c], acc_ref[c]) for c in range(2)]
    for c0 in range(0, pages, chunk):
        v = jnp.concatenate([jnp.concatenate([half(r, 0), half(r, 1)], axis=1)
                             for r in vp_refs[c0:c0 + chunk]], axis=0).astype(BF16)
        for c in range(2):
            k = jnp.concatenate([half(r, c) for r in kp_refs[c0:c0 + chunk]], axis=0).astype(BF16)
            state[c] = _softmax_fold(_dot_nt(qs[c], k) * scale, v, *state[c])

    @pl.when(g < n_steps - 1)
    def _():
        for c in range(2):
            m_ref[c], l_ref[c], acc_ref[c] = state[c]

    @pl.when(g == n_steps - 1)
    def _():
        kn = _pad_keys(kn_ref[...], page).astype(BF16)
        vn = _pad_keys(vn_ref[...], page).astype(BF16)
        visible = _new_key_mask(n_heads * t, t, page)
        for c in range(2):
            s = _dot_nt(qs[c], kn[:, c * dh:(c + 1) * dh]) * scale
            m_ref[c], l_ref[c], acc_ref[c] = _softmax_fold(jnp.where(visible, s, NEG_INF), vn, *state[c])
        _diff_finish(lam_ref, gsub_ref, o_ref, m_ref, l_ref, acc_ref, n_heads, lam_init)


def _diff_sample(z, cache_k, cache_v, layer_slot, page_table, lam_p, g_sub, t, n_heads, dh, lam_init):
    m = z.shape[0]
    n_seq, n_pages = page_table.shape
    page2 = cache_k.shape[2]
    qw = n_heads * 2 * dh
    kcol = qw // (2 * dh)
    pages = _tile(n_pages, PAGES_PER_STEP, 1)
    chunk = _tile(pages, PAGES_PER_CHUNK, 1)
    n_steps = n_pages // pages
    rows = n_heads * t

    def page_spec(p):
        return pl.BlockSpec((None, None, page2, dh),
                            lambda s, g, pt: (layer_slot, pt[s, g * pages + p], 0, 0))

    grid_spec = pltpu.PrefetchScalarGridSpec(
        num_scalar_prefetch=1, grid=(n_seq, n_steps),
        in_specs=[pl.BlockSpec((4, dh), lambda s, g, pt: (0, 0)),
                  pl.BlockSpec((1, 2 * dh), lambda s, g, pt: (0, 0)),
                  pl.BlockSpec((t, qw), lambda s, g, pt: (s, 0)),
                  pl.BlockSpec((t, 2 * dh), lambda s, g, pt: (s, kcol)),
                  pl.BlockSpec((t, 2 * dh), lambda s, g, pt: (s, kcol + 1))]
                 + [page_spec(p) for p in range(pages)] * 2,
        out_specs=pl.BlockSpec((t, qw), lambda s, g, pt: (s, 0)),
        scratch_shapes=[pltpu.VMEM((2, rows, 1), F32), pltpu.VMEM((2, rows, 1), F32),
                        pltpu.VMEM((2, rows, 2 * dh), F32), pltpu.VMEM((2, rows, dh), BF16)])
    return pl.pallas_call(
        functools.partial(_diff_sample_kernel, n_heads=n_heads, dh=dh, lam_init=lam_init,
                          pages=pages, chunk=chunk, n_steps=n_steps),
        grid_spec=grid_spec, out_shape=jax.ShapeDtypeStruct((m, qw), F32),
        compiler_params=_params(2), name="diff_attn_sample",
    )(page_table, lam_p, g_sub.reshape(1, 2 * dh), z, z, z,
      *([cache_k] * pages), *([cache_v] * pages))


def _softcap(x):
    return GATE_CAP * jnp.tanh(x / GATE_CAP)


def _mlstm_kernel(*refs, n_heads, dqk, dv, n_chunks, has_init):
    if has_init:
        (gb_ref, q_ref, k_ref, v_ref, zo_ref, gates_ref, gout_ref, c0_ref, n0_ref, m0_ref,
         y_ref, co_ref, no_ref, mo_ref, c_scr, n_scr, m_scr) = refs
    else:
        (gb_ref, q_ref, k_ref, v_ref, zo_ref, gates_ref, gout_ref,
         y_ref, co_ref, no_ref, mo_ref, c_scr, n_scr, m_scr) = refs
    c = pl.program_id(1)

    @pl.when(c == 0)
    def _():
        if has_init:
            c_scr[...] = c0_ref[...]
            n_scr[...] = n0_ref[...]
            m_scr[...] = m0_ref[...]
        else:
            c_scr[...] = jnp.zeros(c_scr.shape, F32)
            n_scr[...] = jnp.zeros(n_scr.shape, F32)
            m_scr[...] = jnp.zeros(m_scr.shape, F32)

    ln = q_ref.shape[0]
    mm = BF16 if ln >= BF16_ROWS else F32
    row = lax.broadcasted_iota(jnp.int32, (ln, ln), 0)
    col = lax.broadcasted_iota(jnp.int32, (ln, ln), 1)
    causal = col <= row
    diag = col == row
    for h in range(n_heads):
        q = q_ref[:, h * dqk:(h + 1) * dqk]
        k = k_ref[:, h * dqk:(h + 1) * dqk] * (dqk ** -0.5)
        v = v_ref[:, h * dv:(h + 1) * dv]
        ig = _softcap(gates_ref[h] + gb_ref[0, h])
        lf = jax.nn.log_sigmoid(_softcap(gates_ref[n_heads + h] + gb_ref[1, h]))
        bcum_c = jnp.sum(jnp.where(causal, jnp.broadcast_to(lf, (ln, ln)), 0.0), axis=1, keepdims=True)
        bcum_r = jnp.sum(jnp.where(diag, jnp.broadcast_to(bcum_c, (ln, ln)), 0.0), axis=0, keepdims=True)
        ig_c = jnp.sum(jnp.where(diag, jnp.broadcast_to(ig, (ln, ln)), 0.0), axis=1, keepdims=True)
        m_prev = m_scr[h]
        log_w = jnp.where(causal, bcum_c - bcum_r + ig, -jnp.inf)
        log_inter = bcum_c + m_prev
        m_t = jnp.maximum(log_inter, jnp.max(log_w, axis=1, keepdims=True))
        qm, km, vm = q.astype(mm), k.astype(mm), v.astype(mm)
        s = _dot_nt(qm, km) * jnp.exp(log_w - m_t)
        a_inter = jnp.exp(log_inter - m_t)
        c_old = c_scr[h]
        n_old = n_scr[h]
        if ln >= BF16_ROWS:
            q_c = _dot(qm, c_old.astype(BF16))
        else:
            q_pad = jnp.concatenate([q, jnp.zeros((BF16_ROWS - ln, dqk), F32)], axis=0).astype(BF16)
            q_c = _dot(q_pad, c_old.astype(BF16))[:ln]
        num = _dot(s.astype(mm), vm) + a_inter * q_c
        den = jnp.sum(s, axis=1, keepdims=True) + a_inter * jnp.sum(q * n_old, axis=1, keepdims=True)
        hcell = num / jnp.maximum(jnp.abs(den), jnp.exp(-m_t))
        m_new = m_t[ln - 1:ln, :]
        b_last = bcum_c[ln - 1:ln, :]
        decay = jnp.exp(b_last + m_prev - m_new)
        w_in = jnp.exp(b_last - bcum_c + ig_c - m_new)
        kw = w_in * k
        c_scr[h] = decay * c_old + _dot_tn(kw.astype(mm), vm)
        n_scr[h] = decay * n_old + jnp.sum(kw, axis=0, keepdims=True)
        m_scr[h] = m_new
        y = _rms(hcell) * gout_ref[h] * jax.nn.sigmoid(zo_ref[:, h * dv:(h + 1) * dv])
        y_ref[:, h * dv:(h + 1) * dv] = y.astype(y_ref.dtype)

    @pl.when(c == n_chunks - 1)
    def _():
        co_ref[...] = c_scr[...]
        no_ref[...] = n_scr[...]
        mo_ref[...] = m_scr[...]


def _mlstm(z, zg, gate_bias, g_out, n_seq, t, n_heads, dqk, dv, init=None, out_dtype=BF16):
    m = z.shape[0]
    ln = _tile(t, MLSTM_CHUNK, SUBLANES)
    nc = t // ln
    gates = zg[:, :2 * n_heads].T.reshape(2 * n_heads, m // ln, 1, ln)
    wqk, wv = n_heads * dqk, n_heads * dv
    assert (2 * wqk) % wv == 0
    vcol = 2 * wqk // wv
    in_specs = [pl.BlockSpec(memory_space=pltpu.SMEM),
                pl.BlockSpec((ln, wqk), lambda b, c: (b * nc + c, 0)),
                pl.BlockSpec((ln, wqk), lambda b, c: (b * nc + c, 1)),
                pl.BlockSpec((ln, wv), lambda b, c: (b * nc + c, vcol)),
                pl.BlockSpec((ln, wv), lambda b, c: (b * nc + c, vcol + 1)),
                pl.BlockSpec((2 * n_heads, None, 1, ln), lambda b, c: (0, b * nc + c, 0, 0)),
                pl.BlockSpec((n_heads, 1, dv), lambda b, c: (0, 0, 0))]
    args = [gate_bias, z, z, z, z, gates, g_out.reshape(n_heads, 1, dv)]
    state_specs = [pl.BlockSpec((None, n_heads, dqk, dv), lambda b, c: (b, 0, 0, 0)),
                   pl.BlockSpec((None, n_heads, 1, dqk), lambda b, c: (b, 0, 0, 0)),
                   pl.BlockSpec((None, n_heads, 1, 1), lambda b, c: (b, 0, 0, 0))]
    if init is not None:
        c0, n0, m0 = init
        in_specs += state_specs
        args += [c0, n0.reshape(n_seq, n_heads, 1, dqk), m0.reshape(n_seq, n_heads, 1, 1)]
    y, c_new, n_new, m_new = pl.pallas_call(
        functools.partial(_mlstm_kernel, n_heads=n_heads, dqk=dqk, dv=dv, n_chunks=nc, has_init=init is not None),
        grid=(n_seq, nc), in_specs=in_specs,
        out_specs=[pl.BlockSpec((ln, wv), lambda b, c: (b * nc + c, 0))] + state_specs,
        out_shape=[jax.ShapeDtypeStruct((m, wv), out_dtype),
                   jax.ShapeDtypeStruct((n_seq, n_heads, dqk, dv), F32),
                   jax.ShapeDtypeStruct((n_seq, n_heads, 1, dqk), F32),
                   jax.ShapeDtypeStruct((n_seq, n_heads, 1, 1), F32)],
        scratch_shapes=[pltpu.VMEM((n_heads, dqk, dv), F32), pltpu.VMEM((n_heads, 1, dqk), F32),
                        pltpu.VMEM((n_heads, 1, 1), F32)],
        compiler_params=_params(2), name="mlstm",
    )(*args)
    return y, c_new, n_new.reshape(n_seq, n_heads, dqk), m_new.reshape(n_seq, n_heads)


def _mla_post_kernel(z_ref, z2_ref, gq_ref, gkv_ref, cos_ref, sin_ref, cq_ref, ckv_ref, kpe_ref, *, q_lora):
    z = z_ref[...]
    cq_ref[...] = (_rms(z[:, :q_lora]) * gq_ref[...]).astype(cq_ref.dtype)
    ckv_ref[...] = _rms(z[:, q_lora:]) * gkv_ref[...]
    z2 = z2_ref[...]
    r = z2.shape[1] // 2
    kpe_ref[...] = _pos_mul(z2[:, :r], cos_ref[...]) + _pos_mul(z2[:, r:], sin_ref[...])


def _mla_post(z, z2, g_q, g_kv, cos, sin, t):
    m, zc = z.shape
    q_lora, kv_lora, rope = g_q.shape[0], g_kv.shape[0], cos.shape[1]
    tm = _row_tile(m, t, NORM_ROW_TILE)
    return pl.pallas_call(
        functools.partial(_mla_post_kernel, q_lora=q_lora),
        grid=(m // tm, 1),
        in_specs=[pl.BlockSpec((tm, zc), lambda i, j: (i, 0)),
                  pl.BlockSpec((tm, 2 * rope), lambda i, j: (i, 0)),
                  pl.BlockSpec((1, q_lora), lambda i, j: (0, 0)),
                  pl.BlockSpec((1, kv_lora), lambda i, j: (0, 0)),
                  _pos_spec(cos, t, tm, rope), _pos_spec(sin, t, tm, rope)],
        out_specs=[pl.BlockSpec((tm, q_lora), lambda i, j: (i, 0)),
                   pl.BlockSpec((tm, kv_lora), lambda i, j: (i, 0)),
                   pl.BlockSpec((tm, rope), lambda i, j: (i, 0))],
        out_shape=[jax.ShapeDtypeStruct((m, q_lora), BF16),
                   jax.ShapeDtypeStruct((m, kv_lora), F32),
                   jax.ShapeDtypeStruct((m, rope), F32)],
        compiler_params=_params(2), name="mla_post",
    )(z, z2, g_q.reshape(1, q_lora), g_kv.reshape(1, kv_lora), cos, sin)


def _rope_linear_kernel(a_ref, w_ref, wrot_ref, cos_ref, sin_ref, o_ref):
    a = a_ref[...]
    x = _dot(a, w_ref[...].astype(BF16))
    x_rot = _dot(a, wrot_ref[...].astype(BF16))
    o_ref[...] = _pos_mul(x, cos_ref[...]) + _pos_mul(x_rot, sin_ref[...])


def _rope_linear(a, w, w_rot, cos, sin, t):
    m, kdim = a.shape
    n = w.shape[1]
    tm = _row_tile(m, t, ROW_TILE)
    tn = _tile(n, COL_TILE, 128)
    return pl.pallas_call(
        _rope_linear_kernel, grid=(m // tm, n // tn),
        in_specs=[pl.BlockSpec((tm, kdim), lambda i, j: (i, 0)),
                  pl.BlockSpec((kdim, tn), lambda i, j: (0, j)),
                  pl.BlockSpec((kdim, tn), lambda i, j: (0, j)),
                  _pos_spec(cos, t, tm, tn), _pos_spec(sin, t, tm, tn)],
        out_specs=pl.BlockSpec((tm, tn), lambda i, j: (i, j)),
        out_shape=jax.ShapeDtypeStruct((m, n), F32),
        compiler_params=_params(2), name="rope_linear",
    )(a, w, w_rot, cos, sin)


def _mla_prompt_kernel(qi_ref, kj_ref, last_ref, qn_ref, qp_ref, kv_ref, kpe_ref, o_ref, m_ref, l_ref, acc_ref,
                       *, hb, nope, rope, vd):
    pair = pl.program_id(2)
    qi, kj, last = qi_ref[pair], kj_ref[pair], last_ref[pair]
    tq, tk = qn_ref.shape[0], kv_ref.shape[0]
    scale = (nope + rope) ** -0.5

    @pl.when(kj == 0)
    def _():
        _softmax_init(m_ref, l_ref, acc_ref)

    def run(masked):
        kpe = kpe_ref[...].astype(BF16)
        if masked:
            visible = _causal_visible(qi, kj, tq, tk, tq)
        for h in range(hb):
            q = jnp.concatenate([qn_ref[:, h * nope:(h + 1) * nope].astype(BF16),
                                 qp_ref[:, h * rope:(h + 1) * rope].astype(BF16)], axis=1)
            k = jnp.concatenate([kv_ref[:, h * (nope + vd): h * (nope + vd) + nope], kpe], axis=1)
            v = kv_ref[:, h * (nope + vd) + nope: (h + 1) * (nope + vd)]
            s = _dot_nt(q, k) * scale
            if masked:
                s = jnp.where(visible, s, NEG_INF)
            m_ref[h], l_ref[h], acc_ref[h] = _softmax_fold(s, v, m_ref[h], l_ref[h], acc_ref[h])

    @pl.when(last == 0)
    def _():
        run(False)

    @pl.when(last == 1)
    def _():
        run(True)
        for h in range(hb):
            o_ref[:, h * vd:(h + 1) * vd] = (acc_ref[h] / l_ref[h]).astype(o_ref.dtype)


def _mla_prompt(qn, qp, kv, kpe, n_seq, t, n_heads, nope, rope, vd):
    m = qn.shape[0]
    tq = _tile(t, MLA_TQ, BF16_ROWS)
    tk = _tile(t, MLA_TK, tq)
    nq, nk = t // tq, t // tk
    hb = _tile(n_heads, MLA_HEAD_BLOCK, 2)
    tabs = _causal_pairs(nq, tq, tk)
    grid_spec = pltpu.PrefetchScalarGridSpec(
        num_scalar_prefetch=3, grid=(n_seq, n_heads // hb, tabs[0].shape[0]),
        in_specs=[pl.BlockSpec((tq, hb * nope), lambda b, g, p, qi, kj, la: (b * nq + qi[p], g)),
                  pl.BlockSpec((tq, hb * rope), lambda b, g, p, qi, kj, la: (b * nq + qi[p], g)),
                  pl.BlockSpec((tk, hb * (nope + vd)), lambda b, g, p, qi, kj, la: (b * nk + kj[p], g)),
                  pl.BlockSpec((tk, rope), lambda b, g, p, qi, kj, la: (b * nk + kj[p], 0))],
        out_specs=pl.BlockSpec((tq, hb * vd), lambda b, g, p, qi, kj, la: (b * nq + qi[p], g)),
        scratch_shapes=[pltpu.VMEM((hb, tq, 1), F32), pltpu.VMEM((hb, tq, 1), F32),
                        pltpu.VMEM((hb, tq, vd), F32)])
    return pl.pallas_call(
        functools.partial(_mla_prompt_kernel, hb=hb, nope=nope, rope=rope, vd=vd),
        grid_spec=grid_spec, out_shape=jax.ShapeDtypeStruct((m, n_heads * vd), BF16),
        compiler_params=_params(3), name="mla_attn_prompt",
    )(*tabs, qn, qp, kv, kpe)


def _q_latent_kernel(qn_ref, wuk_ref, o_ref):
    x = _dot_nt(qn_ref[...].astype(BF16), wuk_ref[...].astype(BF16))
    o_ref[...] = x.reshape(o_ref.shape)


def _q_latent(qn, w_kvb2, n_seq, t, n_heads, nope, vd):
    m = qn.shape[0]
    kvl = w_kvb2.shape[0]
    assert nope == vd
    return pl.pallas_call(
        _q_latent_kernel, grid=(n_heads,),
        in_specs=[pl.BlockSpec((m, nope), lambda h: (0, h)),
                  pl.BlockSpec((kvl, nope), lambda h: (0, 2 * h))],
        out_specs=pl.BlockSpec((n_seq, None, t, kvl), lambda h: (0, h, 0, 0)),
        out_shape=jax.ShapeDtypeStruct((n_seq, n_heads, t, kvl), F32),
        compiler_params=_params(1), name="mla_q_latent",
    )(qn, w_kvb2)


def _v_up_kernel(ol_ref, wuv_ref, o_ref):
    n_seq, t, kvl = ol_ref.shape
    x = ol_ref[...].reshape(n_seq * t, kvl).astype(BF16)
    o_ref[...] = _dot(x, wuv_ref[...].astype(BF16))


def _v_up(o_lat, w_kvb2, nope, vd):
    n_seq, n_heads, t, kvl = o_lat.shape
    assert nope == vd
    return pl.pallas_call(
        _v_up_kernel, grid=(n_heads,),
        in_specs=[pl.BlockSpec((n_seq, None, t, kvl), lambda h: (0, h, 0, 0)),
                  pl.BlockSpec((kvl, vd), lambda h: (0, 2 * h + 1))],
        out_specs=pl.BlockSpec((n_seq * t, vd), lambda h: (0, h)),
        out_shape=jax.ShapeDtypeStruct((n_seq * t, n_heads * vd), F32),
        compiler_params=_params(1), name="mla_v_up",
    )(o_lat, w_kvb2)


def _mla_sample_kernel(pt_ref, ql_ref, qp_ref, cn_ref, rn_ref, *rest,
                       n_heads, rope, pages, chunk, n_steps, scale):
    del pt_ref
    cp_refs, rp_refs = rest[:pages], rest[pages:2 * pages]
    o_ref, m_ref, l_ref, acc_ref, ql_scr, qp_scr = rest[2 * pages:]
    g = pl.program_id(1)
    _, t, kvl = ql_ref.shape
    rows = n_heads * t

    @pl.when(g == 0)
    def _():
        _softmax_init(m_ref, l_ref, acc_ref)
        ql_scr[...] = ql_ref[...].reshape(rows, kvl).astype(BF16)
        qp = qp_ref[...]
        qp_scr[...] = jnp.concatenate([qp[:, h * rope:(h + 1) * rope] for h in range(n_heads)],
                                      axis=0).astype(BF16)

    q_lat, q_pe = ql_scr[...], qp_scr[...]
    state = (m_ref[0], l_ref[0], acc_ref[0])
    for c0 in range(0, pages, chunk):
        ckv = jnp.concatenate([r[...] for r in cp_refs[c0:c0 + chunk]], axis=0).astype(BF16)
        kpe_t = jnp.concatenate([r[...] for r in rp_refs[c0:c0 + chunk]], axis=1).astype(BF16)
        s = (_dot_nt(q_lat, ckv) + _dot(q_pe, kpe_t)) * scale
        state = _softmax_fold(s, ckv, *state)

    @pl.when(g < n_steps - 1)
    def _():
        m_ref[0], l_ref[0], acc_ref[0] = state

    @pl.when(g == n_steps - 1)
    def _():
        page = cp_refs[0].shape[0]
        cn = _pad_keys(cn_ref[...], page).astype(BF16)
        rn = _pad_keys(rn_ref[...], page).astype(BF16)
        sn = (_dot_nt(q_lat, cn) + _dot_nt(q_pe, rn)) * scale
        _, l_fin, acc_fin = _softmax_fold(jnp.where(_new_key_mask(rows, t, page), sn, NEG_INF), cn, *state)
        o_ref[...] = (acc_fin / l_fin).reshape(o_ref.shape)


def _mla_sample(q_lat, qp, ckv_new, kpe_new, cache_ckv, cache_kpe_t, layer_slot, page_table, nope):
    n_seq, n_heads, t, kvl = q_lat.shape
    rope = kpe_new.shape[1]
    n_pages = page_table.shape[1]
    page = cache_ckv.shape[2]
    pages = _tile(n_pages, MLA_PAGES_PER_STEP, 1)
    chunk = _tile(pages, PAGES_PER_CHUNK, 1)
    n_steps = n_pages // pages
    rows = n_heads * t

    def page_spec(blk, p):
        return pl.BlockSpec((None, None) + blk, lambda s, g, pt: (layer_slot, pt[s, g * pages + p], 0, 0))

    grid_spec = pltpu.PrefetchScalarGridSpec(
        num_scalar_prefetch=1, grid=(n_seq, n_steps),
        in_specs=[pl.BlockSpec((None, n_heads, t, kvl), lambda s, g, pt: (s, 0, 0, 0)),
                  pl.BlockSpec((t, n_heads * rope), lambda s, g, pt: (s, 0)),
                  pl.BlockSpec((t, kvl), lambda s, g, pt: (s, 0)),
                  pl.BlockSpec((t, rope), lambda s, g, pt: (s, 0))]
                 + [page_spec((page, kvl), p) for p in range(pages)]
                 + [page_spec((rope, page), p) for p in range(pages)],
        out_specs=pl.BlockSpec((None, n_heads, t, kvl), lambda s, g, pt: (s, 0, 0, 0)),
        scratch_shapes=[pltpu.VMEM((1, rows, 1), F32), pltpu.VMEM((1, rows, 1), F32),
                        pltpu.VMEM((1, rows, kvl), F32),
                        pltpu.VMEM((rows, kvl), BF16), pltpu.VMEM((rows, rope), BF16)])
    return pl.pallas_call(
        functools.partial(_mla_sample_kernel, n_heads=n_heads, rope=rope, pages=pages, chunk=chunk,
                          n_steps=n_steps, scale=(nope + rope) ** -0.5),
        grid_spec=grid_spec, out_shape=jax.ShapeDtypeStruct((n_seq, n_heads, t, kvl), F32),
        compiler_params=_params(2), name="mla_attn_sample",
    )(page_table, q_lat, qp, ckv_new, kpe_new, *([cache_ckv] * pages), *([cache_kpe_t] * pages))


def _rotate_half_cols(w, rope):
    k, n = w.shape
    w3 = w.reshape(k, n // rope, rope)
    half = rope // 2
    return jnp.concatenate([-w3[..., half:], w3[..., :half]], axis=-1).reshape(k, n)


def _rope_tables(pos, rope, repeat):
    half = rope // 2
    freqs = ROPE_THETA ** (-jnp.arange(half, dtype=F32) / half)
    ang = pos.astype(F32)[:, None] * freqs[None, :]
    cos = jnp.tile(jnp.cos(ang), (1, 2 * repeat))
    sin = jnp.tile(jnp.sin(ang), (1, 2 * repeat))
    return cos, sin


def _diff_lambda_init(layer):
    return 0.8 - 0.6 * math.exp(-0.3 * layer)


def kernel(x_prompt, x_sample, cache_k_a, cache_v_a, state_C_b, state_n_b, state_m_b, cache_ckv_c, cache_kpe_c,
           state_ffn_conv, page_table, c_prompt, c_sample, w_ada, b_ada, ada_table, g_mix, g_ffn, g_final,
           a_w_qkv, a_lambda, a_g_sub, a_w_o, b_w_in, b_gate_bias, b_g_out, b_w_out,
           c_w_in, c_g_q, c_g_kv, c_w_qb, c_w_kvb, c_w_o, f_w_up, f_conv_w, f_conv_b, f_w_down):
    nb, tp, d = x_prompt.shape
    ns, ts, _ = x_sample.shape
    depth = g_mix.shape[0]
    page = cache_k_a.shape[2]
    past_len = page_table.shape[1] * page
    assert ts == SUBLANES and f_conv_w.shape[1] == CONV_W

    dh = a_lambda.shape[-1]
    a_heads = a_w_o.shape[1] // (2 * dh)
    b_heads, dqk, dv = state_C_b.shape[2:]
    kvl, rope, q_lora = cache_ckv_c.shape[-1], cache_kpe_c.shape[-1], c_g_q.shape[-1]
    c_heads = c_w_kvb.shape[2]
    nope = c_w_qb.shape[2] // c_heads - rope
    vd = c_w_kvb.shape[3] - nope
    d_ff = f_conv_w.shape[-1]
    f_pad = -(-d_ff // FFN_PAD) * FFN_PAD

    groups = (("p", nb, tp), ("s", ns, ts))
    x = {"p": x_prompt.reshape(nb * tp, d), "s": x_sample.reshape(ns * ts, d)}

    pad = -nb % SUBLANES
    c_all = jnp.concatenate([c_sample, c_prompt, jnp.zeros((pad, d), F32)], axis=0)
    mod_s, mod_p = _ada(c_all, w_ada, b_ada, ada_table, ns)
    mods = {"p": mod_p[:, :nb].reshape(depth, nb, 1, -1), "s": mod_s}

    pos = {"p": jnp.arange(tp, dtype=jnp.int32), "s": past_len + jnp.arange(ts, dtype=jnp.int32)}
    cache_k = cache_k_a.reshape(cache_k_a.shape[:2] + (2 * page, dh))
    cache_v = cache_v_a.reshape(cache_v_a.shape[:2] + (2 * page, dh))
    cache_kpe_t = jnp.swapaxes(cache_kpe_c, 2, 3)

    outs = {key: {"p": [], "s": []} for key in ("ka", "va", "cb", "nb", "mb", "ckv", "kpe", "cv")}

    for l in range(depth):
        kind, slot = l % N_MIXERS, l // N_MIXERS
        w_down = _cast_pad_rows(f_w_down, l, f_pad)
        if kind == 1:
            n_main = 2 * b_heads * (dqk + dv)
            w_gate = jnp.pad(b_w_in[slot][:, n_main:], ((0, 0), (0, 128 - 2 * b_heads)))
        elif kind == 2:
            w_kpe = c_w_in[slot][:, q_lora + kvl:]
            w_kpe2 = jnp.concatenate([w_kpe, _rotate_half_cols(w_kpe, rope)], axis=1)
            wq3 = c_w_qb[slot].reshape(q_lora, c_heads, nope + rope)
            w_qn = wq3[:, :, :nope].reshape(q_lora, c_heads * nope)
            w_qr = wq3[:, :, nope:].reshape(q_lora, c_heads * rope)
            w_qr_rot = _rotate_half_cols(w_qr, rope)
            w_kvb2 = c_w_kvb[slot].reshape(kvl, c_heads * (nope + vd))

        for key, n_seq, t in groups:
            mod = mods[key]
            xg = x[key]
            h = _norm(xg, g_mix, l, t, mod, 0, 1)
            if kind == 0:
                lam_init = _diff_lambda_init(l)
                z = _linear(h, a_w_qkv, t, layer=slot)
                qw = a_heads * 2 * dh
                k_new, v_new = z[:, qw:qw + 2 * dh], z[:, qw + 2 * dh:]
                outs["ka"][key].append(k_new.reshape(n_seq, t, 1, 2 * dh))
                outs["va"][key].append(v_new.reshape(n_seq, t, 1, 2 * dh))
                if key == "p":
                    o = _diff_prompt(z, a_lambda[slot], a_g_sub[slot], n_seq, t, a_heads, dh, lam_init)
                else:
                    o = _diff_sample(z, cache_k, cache_v, slot, page_table, a_lambda[slot], a_g_sub[slot],
                                     t, a_heads, dh, lam_init).astype(BF16)
                xg = _linear(o, a_w_o, t, layer=slot, residual=(xg, mod, l, 2))
            elif kind == 1:
                z = _linear(h, b_w_in, t, layer=slot, n_out=n_main)
                zg = _linear(h, w_gate, t)
                if key == "p":
                    y, c_new, n_new, m_new = _mlstm(z, zg, b_gate_bias[slot], b_g_out[slot], n_seq, t,
                                                    b_heads, dqk, dv)
                else:
                    y, c_new, n_new, m_new = _mlstm(z, zg, b_gate_bias[slot], b_g_out[slot], n_seq, t,
                                                    b_heads, dqk, dv, out_dtype=F32,
                                                    init=(state_C_b[slot], state_n_b[slot], state_m_b[slot]))
                    y = y.astype(BF16)
                outs["cb"][key].append(c_new)
                outs["nb"][key].append(n_new)
                outs["mb"][key].append(m_new)
                xg = _linear(y, b_w_out, t, layer=slot, residual=(xg, mod, l, 2))
            else:
                z = _linear(h, c_w_in, t, layer=slot, n_out=q_lora + kvl)
                z2 = _linear(h, w_kpe2, t)
                cos1, sin1 = _rope_tables(pos[key], rope, 1)
                cq, ckv, kpe = _mla_post(z, z2, c_g_q[slot], c_g_kv[slot], cos1, sin1, t)
                outs["ckv"][key].append(ckv.reshape(n_seq, t, kvl))
                outs["kpe"][key].append(kpe.reshape(n_seq, t, rope))
                cosh, sinh = _rope_tables(pos[key], rope, c_heads)
                qn = _linear(cq, w_qn, t)
                qp = _rope_linear(cq, w_qr, w_qr_rot, cosh, sinh, t)
                if key == "p":
                    kv = _linear(ckv, w_kvb2, t, out_dtype=BF16)
                    o = _mla_prompt(qn, qp, kv, kpe, n_seq, t, c_heads, nope, rope, vd)
                else:
                    q_lat = _q_latent(qn, w_kvb2, n_seq, t, c_heads, nope, vd)
                    o_lat = _mla_sample(q_lat, qp, ckv, kpe, cache_ckv_c, cache_kpe_t, slot, page_table, nope)
                    o = _v_up(o_lat, w_kvb2, nope, vd).astype(BF16)
                xg = _linear(o, c_w_o, t, layer=slot, k_steps=2, residual=(xg, mod, l, 2))

            h = _norm(xg, g_ffn, l, t, mod, 3, 4)
            prev = state_ffn_conv[l] if key == "s" else None
            g, tail = _ffn_up(h, f_w_up, f_conv_w, f_conv_b, l, t, prev)
            outs["cv"][key].append(tail.reshape(n_seq, SUBLANES, -1)[:, SUBLANES - (CONV_W - 1):, :d_ff])
            x[key] = _linear(g, w_down, t, tn=DOWN_COL_TILE, k_steps=FFN_DOWN_K_STEPS,
                             residual=(xg, mod, l, 5))

    y_p = _norm(x["p"], g_final, 0, tp, out_dtype=F32).reshape(nb, tp, d)
    y_s = _norm(x["s"], g_final, 0, ts, out_dtype=F32).reshape(ns, ts, d)
    res = [y_p, y_s]
    for key in ("ka", "va", "cb", "nb", "mb", "ckv", "kpe", "cv"):
        res += [jnp.stack(outs[key]["p"]), jnp.stack(outs[key]["s"])]
    return tuple(res)
```

```python
import functools
import math

import jax
import jax.numpy as jnp
from jax import lax
from jax.experimental import pallas as pl
from jax.experimental.pallas import tpu as pltpu

F32 = jnp.float32
BF16 = jnp.bfloat16

EPS = 1e-6
NEG_INF = -1e30
LOG2E = math.log2(math.e)
GATE_CAP = 15.0
ROPE_THETA = 10000.0
N_MIXERS = 3
CONV_W = 3

V7X_VMEM_LIMIT_BYTES = 56 << 20
SUBLANES = 8
BF16_ROWS = 16

ROW_TILE = 1024
NORM_ROW_TILE = 512
COL_TILE = 512
DOWN_COL_TILE = 1024
KV_UP_COL_TILE = 2048
FFN_COL_TILE = 256
FFN_PAD = 1024
FFN_DOWN_K_STEPS = 4
CAST_ROW_TILE = 256
MLSTM_CHUNK = 128
DIFF_TQ = 256
DIFF_TK = 1024
DIFF_HEADS_PER_ITER = 1
DIFF_UNROLL = 2
MLA_TQ = 256
MLA_TK = 1024
MLA_HEAD_BLOCK = 4
PAGES_PER_STEP = 16
MLA_PAGES_PER_STEP = 16
PAGES_PER_CHUNK = 16


def _tile(n, pref, align):
    if n <= pref:
        return n
    t = (pref // align) * align
    while t >= align:
        if n % t == 0:
            return t
        t -= align
    raise ValueError(f"no tile for {n} (pref {pref}, align {align})")


def _params(n_axes):
    return pltpu.CompilerParams(dimension_semantics=("arbitrary",) * n_axes,
                                vmem_limit_bytes=V7X_VMEM_LIMIT_BYTES)


def _dot(a, b):
    return jnp.dot(a, b, preferred_element_type=F32)


def _dot_nt(a, b):
    return lax.dot_general(a, b, (((1,), (1,)), ((), ())), preferred_element_type=F32)


def _dot_tn(a, b):
    return lax.dot_general(a, b, (((0,), (0,)), ((), ())), preferred_element_type=F32)


def _w_spec(w, layer, blk, idx):
    if w.ndim == len(blk) + 1:
        return pl.BlockSpec((None,) + tuple(blk), lambda *g: (layer,) + tuple(idx(*g)))
    return pl.BlockSpec(tuple(blk), idx)


def _row_tile(m, t, pref):
    if t >= BF16_ROWS:
        return _tile(t, pref, BF16_ROWS)
    return _tile(m, pref, BF16_ROWS)


def _vec_spec(vec, layer, t, tm, tn, col_blk):
    if vec.ndim == 4:
        assert t % tm == 0
        return pl.BlockSpec((None, 1, 1, tn), lambda i, *g: (layer, i // (t // tm), 0, col_blk(*g)))
    assert tm % t == 0
    return pl.BlockSpec((None, tm // t, tn), lambda i, *g: (layer, i, col_blk(*g)))


def _vec(ref):
    return ref[0] if len(ref.shape) == 3 else ref[...]


def _seq_apply(x, v, fn):
    g = v.shape[0]
    if g == 1:
        return fn(x, v)
    tm, n = x.shape
    return fn(x.reshape(g, tm // g, n), v[:, None, :]).reshape(tm, n)


def _pos_mul(x, tab):
    if tab.shape[0] == x.shape[0]:
        return x * tab
    tm, n = x.shape
    return (x.reshape(tm // tab.shape[0], tab.shape[0], n) * tab[None]).reshape(tm, n)


def _pos_spec(tab, t, tm, tn):
    if t >= tm:
        return pl.BlockSpec((tm, tn), lambda i, j, *g: (i % (t // tm), j))
    return pl.BlockSpec((t, tn), lambda i, j, *g: (0, j))


def _rms(x):
    return x * lax.rsqrt(jnp.mean(x * x, axis=-1, keepdims=True) + EPS)


def _ada_kernel(c_ref, w_ref, b_ref, tab_ref, os_ref, op_ref, *, depth, n_sample):
    c = c_ref[...]
    a = (c * jax.nn.sigmoid(c)).astype(BF16)
    base = _dot(a, w_ref[...].astype(BF16)) + b_ref[...]
    for l in range(depth):
        full = base + tab_ref[l]
        os_ref[l] = full[:n_sample]
        op_ref[l] = full[n_sample:]


def _ada(c_all, w_ada, b_ada, ada_table, n_sample):
    r, d = c_all.shape
    n = w_ada.shape[1]
    depth = ada_table.shape[0]
    tn = _tile(n, COL_TILE, 128)
    return pl.pallas_call(
        functools.partial(_ada_kernel, depth=depth, n_sample=n_sample),
        grid=(n // tn,),
        in_specs=[pl.BlockSpec((r, d), lambda j: (0, 0)),
                  pl.BlockSpec((d, tn), lambda j: (0, j)),
                  pl.BlockSpec((1, tn), lambda j: (0, j)),
                  pl.BlockSpec((depth, 1, tn), lambda j: (0, 0, j))],
        out_specs=[pl.BlockSpec((depth, n_sample, tn), lambda j: (0, 0, j)),
                   pl.BlockSpec((depth, r - n_sample, tn), lambda j: (0, 0, j))],
        out_shape=[jax.ShapeDtypeStruct((depth, n_sample, n), F32),
                   jax.ShapeDtypeStruct((depth, r - n_sample, n), F32)],
        compiler_params=_params(1), name="ada",
    )(c_all, w_ada, b_ada.reshape(1, n), ada_table.reshape(depth, 1, n))


def _norm_kernel(*refs, modulated):
    if modulated:
        x_ref, g_ref, sh_ref, sc_ref, o_ref = refs
    else:
        x_ref, g_ref, o_ref = refs
    y = _rms(x_ref[...]) * g_ref[...]
    if modulated:
        y = _seq_apply(y, _vec(sc_ref), lambda a, s: a * (1.0 + s))
        y = _seq_apply(y, _vec(sh_ref), lambda a, s: a + s)
    o_ref[...] = y.astype(o_ref.dtype)


def _norm(x, g, layer, t, mod=None, shift_chunk=0, scale_chunk=1, out_dtype=BF16):
    m, d = x.shape
    tm = _row_tile(m, t, NORM_ROW_TILE)
    g = g.reshape(g.shape[:-1] + (1, d))
    in_specs = [pl.BlockSpec((tm, d), lambda i: (i, 0)), _w_spec(g, layer, (1, d), lambda i: (0, 0))]
    args = [x, g]
    if mod is not None:
        in_specs += [_vec_spec(mod, layer, t, tm, d, lambda: shift_chunk),
                     _vec_spec(mod, layer, t, tm, d, lambda: scale_chunk)]
        args += [mod, mod]
    return pl.pallas_call(
        functools.partial(_norm_kernel, modulated=mod is not None),
        grid=(m // tm,), in_specs=in_specs,
        out_specs=pl.BlockSpec((tm, d), lambda i: (i, 0)),
        out_shape=jax.ShapeDtypeStruct((m, d), out_dtype),
        compiler_params=_params(1), name="norm",
    )(*args)


def _linear_kernel(*refs, nk, residual):
    if residual:
        a_ref, w_ref, x_ref, gate_ref, o_ref, *scratch = refs
    else:
        a_ref, w_ref, o_ref, *scratch = refs
    part = _dot(a_ref[...].astype(BF16), w_ref[...].astype(BF16))

    def finish(acc):
        if residual:
            acc = x_ref[...] + _seq_apply(acc, _vec(gate_ref), lambda a, s: s * a)
        o_ref[...] = acc.astype(o_ref.dtype)

    if nk == 1:
        finish(part)
        return
    (acc_ref,) = scratch
    k = pl.program_id(2)

    @pl.when(k == 0)
    def _():
        acc_ref[...] = part

    @pl.when(k > 0)
    def _():
        acc_ref[...] += part

    @pl.when(k == nk - 1)
    def _():
        finish(acc_ref[...])


def _linear(a, w, t, *, layer=0, n_out=None, col0=0, tn=COL_TILE, k_steps=1, out_dtype=F32, residual=None):
    m, kdim = a.shape
    n_out = n_out or w.shape[-1]
    tm = _row_tile(m, t, ROW_TILE)
    tn = _tile(n_out, tn, 128)
    assert col0 % tn == 0 and kdim % k_steps == 0 and w.shape[-2] == kdim
    tk = kdim // k_steps
    cb = col0 // tn
    in_specs = [pl.BlockSpec((tm, tk), lambda i, j, k: (i, k)),
                _w_spec(w, layer, (tk, tn), lambda i, j, k: (k, cb + j))]
    args = [a, w]
    if residual is not None:
        x, gate, mod_layer, chunk = residual
        assert x.shape == (m, n_out)
        in_specs += [pl.BlockSpec((tm, tn), lambda i, j, k: (i, j)),
                     _vec_spec(gate, mod_layer, t, tm, tn, lambda j, k: chunk * (n_out // tn) + j)]
        args += [x, gate]
    return pl.pallas_call(
        functools.partial(_linear_kernel, nk=k_steps, residual=residual is not None),
        grid=(m // tm, n_out // tn, k_steps), in_specs=in_specs,
        out_specs=pl.BlockSpec((tm, tn), lambda i, j, k: (i, j)),
        out_shape=jax.ShapeDtypeStruct((m, n_out), out_dtype),
        scratch_shapes=[pltpu.VMEM((tm, tn), F32)] if k_steps > 1 else [],
        compiler_params=_params(3), name="linear",
    )(*args)


def _cast_pad_kernel(w_ref, o_ref, *, n_valid):
    i = pl.program_id(0)
    o_ref[...] = jnp.where(i < n_valid, w_ref[...], 0.0).astype(o_ref.dtype)


def _cast_pad_rows(w, layer, rows_pad):
    _, k, n = w.shape
    tr = _tile(math.gcd(k, rows_pad), CAST_ROW_TILE, BF16_ROWS)
    n_valid = k // tr
    return pl.pallas_call(
        functools.partial(_cast_pad_kernel, n_valid=n_valid),
        grid=(rows_pad // tr,),
        in_specs=[pl.BlockSpec((None, tr, n), lambda i: (layer, jnp.minimum(i, n_valid - 1), 0))],
        out_specs=pl.BlockSpec((tr, n), lambda i: (i, 0)),
        out_shape=jax.ShapeDtypeStruct((rows_pad, n), BF16),
        compiler_params=_params(1), name="cast_pad",
    )(w)


def _ffn_up_kernel(*refs, t, n_valid, has_prev):
    if has_prev:
        h_ref, wa_ref, wu_ref, cw_ref, cb_ref, e1_ref, e2_ref, g_ref, tail_ref = refs
    else:
        h_ref, wa_ref, wu_ref, cw_ref, cb_ref, g_ref, tail_ref = refs
    j = pl.program_id(1)
    h = h_ref[...]
    a = _dot(h, wa_ref[...].astype(BF16))
    u = _dot(h, wu_ref[...].astype(BF16))
    tm = a.shape[0]
    r = lax.broadcasted_iota(jnp.int32, a.shape, 0)
    if tm != t:
        r = lax.rem(r, t)
    back1 = pltpu.roll(a, 1, 0)
    back2 = pltpu.roll(a, 2, 0)
    back1 = jnp.where(r >= 1, back1, e1_ref[...] if has_prev else 0.0)
    back2 = jnp.where(r >= 2, back2, e2_ref[...] if has_prev else 0.0)
    cw = cw_ref[...]
    ac = cb_ref[...] + cw[0:1] * back2 + cw[1:2] * back1 + cw[2:3] * a
    g = ac * jax.nn.sigmoid(ac) * u
    g_ref[...] = jnp.where(j < n_valid, g, 0.0).astype(g_ref.dtype)
    tail_ref[...] = a[tm - SUBLANES:, :] if t == tm else a


def _ffn_up(h, w_up, conv_w, conv_b, layer, t, prev=None):
    m, d = h.shape
    f = conv_w.shape[-1]
    tm = t if t >= BF16_ROWS else m
    assert t == tm or t == SUBLANES
    tn = _tile(f, FFN_COL_TILE, 128)
    n_valid = f // tn
    f_pad = -(-f // FFN_PAD) * FFN_PAD
    n_tiles = f_pad // tn
    tail_rows = SUBLANES if t == tm else tm

    def col(j):
        return jnp.minimum(j, n_valid - 1)

    conv_b = conv_b.reshape(conv_b.shape[0], 1, f)
    in_specs = [pl.BlockSpec((tm, d), lambda i, j: (i, 0), pipeline_mode=pl.Buffered(1)),
                _w_spec(w_up, layer, (d, tn), lambda i, j: (0, col(j))),
                _w_spec(w_up, layer, (d, tn), lambda i, j: (0, n_valid + col(j))),
                _w_spec(conv_w, layer, (CONV_W, tn), lambda i, j: (0, col(j))),
                _w_spec(conv_b, layer, (1, tn), lambda i, j: (0, col(j)))]
    args = [h, w_up, w_up, conv_w, conv_b]
    if prev is not None:
        n_seq = m // t
        zeros = jnp.zeros((n_seq, t - 2, f), F32)
        e2 = jnp.concatenate([prev, zeros], axis=1).reshape(m, f)
        e1 = jnp.concatenate([prev[:, 1:2], jnp.zeros((n_seq, 1, f), F32), zeros], axis=1).reshape(m, f)
        in_specs += [pl.BlockSpec((tm, tn), lambda i, j: (i, col(j)))] * 2
        args += [e1, e2]
    return pl.pallas_call(
        functools.partial(_ffn_up_kernel, t=t, n_valid=n_valid, has_prev=prev is not None),
        grid=(m // tm, n_tiles), in_specs=in_specs,
        out_specs=[pl.BlockSpec((tm, tn), lambda i, j: (i, j)),
                   pl.BlockSpec((tail_rows, tn), lambda i, j: (i, j))],
        out_shape=[jax.ShapeDtypeStruct((m, f_pad), BF16),
                   jax.ShapeDtypeStruct((m // tm * tail_rows, f_pad), F32)],
        compiler_params=_params(2), name="ffn_up",
    )(*args)


def _softmax_fold(s, v, m_old, l_old, acc_old):
    m_new = jnp.maximum(m_old, jnp.max(s, axis=1, keepdims=True))
    p = jnp.exp2(s - m_new)
    alpha = jnp.exp2(m_old - m_new)
    l_new = alpha * l_old + jnp.sum(p, axis=1, keepdims=True)
    acc_new = alpha * acc_old + _dot(p.astype(BF16), v)
    return m_new, l_new, acc_new


def _softmax_local(s, v):
    m = jnp.max(s, axis=1, keepdims=True)
    p = jnp.exp2(s - m)
    return m, jnp.sum(p, axis=1, keepdims=True), _dot(p.astype(BF16), v)


def _softmax_merge(state, parts):
    m_old, l_old, acc_old = state
    m_new = m_old
    for m, _, _ in parts:
        m_new = jnp.maximum(m_new, m)
    alpha = jnp.exp2(m_old - m_new)
    l_new, acc_new = alpha * l_old, alpha * acc_old
    for m, l, acc in parts:
        w = jnp.exp2(m - m_new)
        l_new = l_new + w * l
        acc_new = acc_new + w * acc
    return m_new, l_new, acc_new


def _softmax_init(m_ref, l_ref, acc_ref):
    m_ref[...] = jnp.full(m_ref.shape, NEG_INF, F32)
    l_ref[...] = jnp.zeros(l_ref.shape, F32)
    acc_ref[...] = jnp.zeros(acc_ref.shape, F32)


def _causal_pairs(nq, tq, tk):
    qi, kj, last = [], [], []
    for q in range(nq):
        n_kv = -(-(q + 1) * tq // tk)
        for k in range(n_kv):
            qi.append(q)
            kj.append(k)
            last.append(int(k == n_kv - 1))
    return tuple(jnp.asarray(a, jnp.int32) for a in (qi, kj, last))


def _causal_visible(qi, kj, tq, tk, rows):
    row = lax.rem(lax.broadcasted_iota(jnp.int32, (rows, tk), 0), tq)
    col = lax.broadcasted_iota(jnp.int32, (rows, tk), 1)
    return kj * tk + col <= qi * tq + row


def _stack_heads(zq, n_heads, width, offset):
    return jnp.concatenate(
        [zq[:, h * 2 * width + offset: h * 2 * width + offset + width] for h in range(n_heads)], axis=0)


def _diff_lambda(lam_ref, lam_init):
    lp = lam_ref[...]
    return (jnp.exp(jnp.sum(lp[0:1] * lp[1:2], axis=1, keepdims=True))
            - jnp.exp(jnp.sum(lp[2:3] * lp[3:4], axis=1, keepdims=True)) + lam_init)


def _diff_finish(lam_ref, gsub_ref, o_ref, m_ref, l_ref, acc_ref, n_heads, lam_init):
    lam = _diff_lambda(lam_ref, lam_init)
    tq = acc_ref.shape[1] // n_heads
    w = acc_ref.shape[2]
    for h in range(n_heads):
        rows = slice(h * tq, (h + 1) * tq)
        o = acc_ref[0, rows, :] / l_ref[0, rows, :] - lam * (acc_ref[1, rows, :] / l_ref[1, rows, :])
        y = _rms(o) * gsub_ref[...] * (1.0 - lam_init)
        o_ref[:, h * w:(h + 1) * w] = y.astype(o_ref.dtype)


def _diff_prompt_kernel(qi_ref, kj_ref, last_ref, lam_ref, gsub_ref, zq_ref, k_ref, v_ref, o_ref,
                        m_ref, l_ref, acc_ref, q_scr, k_scr, v_scr, *, n_heads, dh, lam_init, hpi, unroll):
    pair = pl.program_id(1)
    qi, kj, last = qi_ref[pair], kj_ref[pair], last_ref[pair]
    tq, tk = zq_ref.shape[0], k_ref.shape[0]

    @pl.when(kj == 0)
    def _():
        _softmax_init(m_ref, l_ref, acc_ref)
        zq = zq_ref[...]
        for c in range(2):
            q_scr[c] = _stack_heads(zq, n_heads, dh, c * dh).astype(BF16)

    k_scr[...] = k_ref[...].astype(BF16)
    v_scr[...] = v_ref[...].astype(BF16)

    def run(masked):
        rc = hpi * tq
        if masked:
            visible = _causal_visible(qi, kj, tq, tk, rc)

        def body(i, carry):
            rows = pl.ds(pl.multiple_of(i * rc, rc), rc)
            v = v_scr[...]
            for c in range(2):
                s = _dot_nt(q_scr[c, rows, :], k_scr[:, c * dh:(c + 1) * dh]) * (dh ** -0.5 * LOG2E)
                if masked:
                    s = jnp.where(visible, s, NEG_INF)
                m_new, l_new, acc_new = _softmax_fold(s, v, m_ref[c, rows, :], l_ref[c, rows, :],
                                                      acc_ref[c, rows, :])
                m_ref[c, rows, :] = m_new
                l_ref[c, rows, :] = l_new
                acc_ref[c, rows, :] = acc_new
            return carry

        lax.fori_loop(0, n_heads // hpi, body, 0, unroll=unroll)

    @pl.when(last == 0)
    def _():
        run(False)

    @pl.when(last == 1)
    def _():
        run(True)
        _diff_finish(lam_ref, gsub_ref, o_ref, m_ref, l_ref, acc_ref, n_heads, lam_init)


def _diff_prompt(z, lam_p, g_sub, n_seq, t, n_heads, dh, lam_init):
    m = z.shape[0]
    qw = n_heads * 2 * dh
    tq = _tile(t, DIFF_TQ, BF16_ROWS)
    tk = _tile(t, DIFF_TK, tq)
    nq, nk = t // tq, t // tk
    kcol = qw // (2 * dh)
    rows = n_heads * tq
    tabs = _causal_pairs(nq, tq, tk)
    grid_spec = pltpu.PrefetchScalarGridSpec(
        num_scalar_prefetch=3, grid=(n_seq, tabs[0].shape[0]),
        in_specs=[pl.BlockSpec((4, dh), lambda b, p, qi, kj, la: (0, 0)),
                  pl.BlockSpec((1, 2 * dh), lambda b, p, qi, kj, la: (0, 0)),
                  pl.BlockSpec((tq, qw), lambda b, p, qi, kj, la: (b * nq + qi[p], 0)),
                  pl.BlockSpec((tk, 2 * dh), lambda b, p, qi, kj, la: (b * nk + kj[p], kcol)),
                  pl.BlockSpec((tk, 2 * dh), lambda b, p, qi, kj, la: (b * nk + kj[p], kcol + 1))],
        out_specs=pl.BlockSpec((tq, qw), lambda b, p, qi, kj, la: (b * nq + qi[p], 0)),
        scratch_shapes=[pltpu.VMEM((2, rows, 1), F32), pltpu.VMEM((2, rows, 1), F32),
                        pltpu.VMEM((2, rows, 2 * dh), F32), pltpu.VMEM((2, rows, dh), BF16),
                        pltpu.VMEM((tk, 2 * dh), BF16), pltpu.VMEM((tk, 2 * dh), BF16)])
    return pl.pallas_call(
        functools.partial(_diff_prompt_kernel, n_heads=n_heads, dh=dh, lam_init=lam_init,
                          hpi=_tile(n_heads, DIFF_HEADS_PER_ITER, 1), unroll=DIFF_UNROLL),
        grid_spec=grid_spec, out_shape=jax.ShapeDtypeStruct((m, qw), BF16),
        compiler_params=_params(2), name="diff_attn_prompt",
    )(*tabs, lam_p, g_sub.reshape(1, 2 * dh), z, z, z)


def _pad_keys(x, n):
    return jnp.concatenate([x, jnp.zeros((n - x.shape[0], x.shape[1]), x.dtype)], axis=0)


def _new_key_mask(rows, t, n_keys):
    tok = lax.rem(lax.broadcasted_iota(jnp.int32, (rows, n_keys), 0), t)
    key = lax.broadcasted_iota(jnp.int32, (rows, n_keys), 1)
    return key <= tok


def _diff_sample_kernel(pt_ref, lam_ref, gsub_ref, zq_ref, kn_ref, vn_ref, *rest,
                        n_heads, dh, lam_init, pages, chunk, n_steps):
    del pt_ref
    kp_refs, vp_refs = rest[:pages], rest[pages:2 * pages]
    o_ref, m_ref, l_ref, acc_ref, q_scr = rest[2 * pages:]
    g = pl.program_id(1)
    t = zq_ref.shape[0]
    page = kp_refs[0].shape[0] // 2
    scale = dh ** -0.5 * LOG2E

    @pl.when(g == 0)
    def _():
        _softmax_init(m_ref, l_ref, acc_ref)
        zq = zq_ref[...]
        for c in range(2):
            q_scr[c] = _stack_heads(zq, n_heads, dh, c * dh).astype(BF16)

    def half(ref, c):
        return ref[pl.ds(c, page, stride=2), :]

    qs = [q_scr[0], q_scr[1]]
    parts = [[], []]
    for c0 in range(0, pages, chunk):
        v = jnp.concatenate([jnp.concatenate([half(r, 0), half(r, 1)], axis=1)
                             for r in vp_refs[c0:c0 + chunk]], axis=0).astype(BF16)
        for c in range(2):
            k = jnp.concatenate([half(r, c) for r in kp_refs[c0:c0 + chunk]], axis=0).astype(BF16)
            parts[c].append(_softmax_local(_dot_nt(qs[c], k) * scale, v))
    state = [_softmax_merge((m_ref[c], l_ref[c], acc_ref[c]), parts[c]) for c in range(2)]

    @pl.when(g < n_steps - 1)
    def _():
        for c in range(2):
            m_ref[c], l_ref[c], acc_ref[c] = state[c]

    @pl.when(g == n_steps - 1)
    def _():
        kn = _pad_keys(kn_ref[...], page).astype(BF16)
        vn = _pad_keys(vn_ref[...], page).astype(BF16)
        visible = _new_key_mask(n_heads * t, t, page)
        for c in range(2):
            s = _dot_nt(qs[c], kn[:, c * dh:(c + 1) * dh]) * scale
            m_ref[c], l_ref[c], acc_ref[c] = _softmax_fold(jnp.where(visible, s, NEG_INF), vn, *state[c])
        _diff_finish(lam_ref, gsub_ref, o_ref, m_ref, l_ref, acc_ref, n_heads, lam_init)


def _diff_sample(z, cache_k, cache_v, layer_slot, page_table, lam_p, g_sub, t, n_heads, dh, lam_init):
    m = z.shape[0]
    n_seq, n_pages = page_table.shape
    page2 = cache_k.shape[2]
    qw = n_heads * 2 * dh
    kcol = qw // (2 * dh)
    pages = _tile(n_pages, PAGES_PER_STEP, 1)
    chunk = _tile(pages, PAGES_PER_CHUNK, 1)
    n_steps = n_pages // pages
    rows = n_heads * t

    def page_spec(p):
        return pl.BlockSpec((None, None, page2, dh),
                            lambda s, g, pt: (layer_slot, pt[s, g * pages + p], 0, 0))

    grid_spec = pltpu.PrefetchScalarGridSpec(
        num_scalar_prefetch=1, grid=(n_seq, n_steps),
        in_specs=[pl.BlockSpec((4, dh), lambda s, g, pt: (0, 0)),
                  pl.BlockSpec((1, 2 * dh), lambda s, g, pt: (0, 0)),
                  pl.BlockSpec((t, qw), lambda s, g, pt: (s, 0)),
                  pl.BlockSpec((t, 2 * dh), lambda s, g, pt: (s, kcol)),
                  pl.BlockSpec((t, 2 * dh), lambda s, g, pt: (s, kcol + 1))]
                 + [page_spec(p) for p in range(pages)] * 2,
        out_specs=pl.BlockSpec((t, qw), lambda s, g, pt: (s, 0)),
        scratch_shapes=[pltpu.VMEM((2, rows, 1), F32), pltpu.VMEM((2, rows, 1), F32),
                        pltpu.VMEM((2, rows, 2 * dh), F32), pltpu.VMEM((2, rows, dh), BF16)])
    return pl.pallas_call(
        functools.partial(_diff_sample_kernel, n_heads=n_heads, dh=dh, lam_init=lam_init,
                          pages=pages, chunk=chunk, n_steps=n_steps),
        grid_spec=grid_spec, out_shape=jax.ShapeDtypeStruct((m, qw), F32),
        compiler_params=_params(2), name="diff_attn_sample",
    )(page_table, lam_p, g_sub.reshape(1, 2 * dh), z, z, z,
      *([cache_k] * pages), *([cache_v] * pages))


def _softcap(x):
    return GATE_CAP * jnp.tanh(x / GATE_CAP)


def _mlstm_kernel(*refs, n_heads, dqk, dv, n_chunks, has_init):
    if has_init:
        (gb_ref, q_ref, k_ref, v_ref, zo_ref, gates_ref, gout_ref, c0_ref, n0_ref, m0_ref,
         y_ref, co_ref, no_ref, mo_ref, c_scr, n_scr, m_scr) = refs
    else:
        (gb_ref, q_ref, k_ref, v_ref, zo_ref, gates_ref, gout_ref,
         y_ref, co_ref, no_ref, mo_ref, c_scr, n_scr, m_scr) = refs
    c = pl.program_id(1)

    @pl.when(c == 0)
    def _():
        if has_init:
            c_scr[...] = c0_ref[...]
            n_scr[...] = n0_ref[...]
            m_scr[...] = m0_ref[...]
        else:
            c_scr[...] = jnp.zeros(c_scr.shape, F32)
            n_scr[...] = jnp.zeros(n_scr.shape, F32)
            m_scr[...] = jnp.zeros(m_scr.shape, F32)

    ln = q_ref.shape[0]
    mm = BF16 if ln >= BF16_ROWS else F32
    row = lax.broadcasted_iota(jnp.int32, (ln, ln), 0)
    col = lax.broadcasted_iota(jnp.int32, (ln, ln), 1)
    causal = col <= row
    diag = col == row
    for h in range(n_heads):
        q = q_ref[:, h * dqk:(h + 1) * dqk]
        k = k_ref[:, h * dqk:(h + 1) * dqk] * (dqk ** -0.5)
        v = v_ref[:, h * dv:(h + 1) * dv]
        ig = _softcap(gates_ref[h] + gb_ref[0, h])
        lf = jax.nn.log_sigmoid(_softcap(gates_ref[n_heads + h] + gb_ref[1, h]))
        bcum_c = jnp.sum(jnp.where(causal, jnp.broadcast_to(lf, (ln, ln)), 0.0), axis=1, keepdims=True)
        bcum_r = jnp.sum(jnp.where(diag, jnp.broadcast_to(bcum_c, (ln, ln)), 0.0), axis=0, keepdims=True)
        ig_c = jnp.sum(jnp.where(diag, jnp.broadcast_to(ig, (ln, ln)), 0.0), axis=1, keepdims=True)
        m_prev = m_scr[h]
        log_w = jnp.where(causal, bcum_c - bcum_r + ig, -jnp.inf)
        log_inter = bcum_c + m_prev
        m_t = jnp.maximum(log_inter, jnp.max(log_w, axis=1, keepdims=True))
        qm, km, vm = q.astype(mm), k.astype(mm), v.astype(mm)
        s = _dot_nt(qm, km) * jnp.exp(log_w - m_t)
        a_inter = jnp.exp(log_inter - m_t)
        c_old = c_scr[h]
        n_old = n_scr[h]
        if ln >= BF16_ROWS:
            q_c = _dot(qm, c_old.astype(BF16))
        else:
            q_pad = jnp.concatenate([q, jnp.zeros((BF16_ROWS - ln, dqk), F32)], axis=0).astype(BF16)
            q_c = _dot(q_pad, c_old.astype(BF16))[:ln]
        num = _dot(s.astype(mm), vm) + a_inter * q_c
        den = jnp.sum(s, axis=1, keepdims=True) + a_inter * jnp.sum(q * n_old, axis=1, keepdims=True)
        hcell = num / jnp.maximum(jnp.abs(den), jnp.exp(-m_t))
        m_new = m_t[ln - 1:ln, :]
        b_last = bcum_c[ln - 1:ln, :]
        decay = jnp.exp(b_last + m_prev - m_new)
        w_in = jnp.exp(b_last - bcum_c + ig_c - m_new)
        kw = w_in * k
        c_scr[h] = decay * c_old + _dot_tn(kw.astype(mm), vm)
        n_scr[h] = decay * n_old + jnp.sum(kw, axis=0, keepdims=True)
        m_scr[h] = m_new
        y = _rms(hcell) * gout_ref[h] * jax.nn.sigmoid(zo_ref[:, h * dv:(h + 1) * dv])
        y_ref[:, h * dv:(h + 1) * dv] = y.astype(y_ref.dtype)

    @pl.when(c == n_chunks - 1)
    def _():
        co_ref[...] = c_scr[...]
        no_ref[...] = n_scr[...]
        mo_ref[...] = m_scr[...]


def _mlstm(z, zg, gate_bias, g_out, n_seq, t, n_heads, dqk, dv, init=None, out_dtype=BF16):
    m = z.shape[0]
    ln = _tile(t, MLSTM_CHUNK, SUBLANES)
    nc = t // ln
    gates = zg[:, :2 * n_heads].T.reshape(2 * n_heads, m // ln, 1, ln)
    wqk, wv = n_heads * dqk, n_heads * dv
    assert (2 * wqk) % wv == 0
    vcol = 2 * wqk // wv
    in_specs = [pl.BlockSpec(memory_space=pltpu.SMEM),
                pl.BlockSpec((ln, wqk), lambda b, c: (b * nc + c, 0)),
                pl.BlockSpec((ln, wqk), lambda b, c: (b * nc + c, 1)),
                pl.BlockSpec((ln, wv), lambda b, c: (b * nc + c, vcol)),
                pl.BlockSpec((ln, wv), lambda b, c: (b * nc + c, vcol + 1)),
                pl.BlockSpec((2 * n_heads, None, 1, ln), lambda b, c: (0, b * nc + c, 0, 0)),
                pl.BlockSpec((n_heads, 1, dv), lambda b, c: (0, 0, 0))]
    args = [gate_bias, z, z, z, z, gates, g_out.reshape(n_heads, 1, dv)]
    state_specs = [pl.BlockSpec((None, n_heads, dqk, dv), lambda b, c: (b, 0, 0, 0)),
                   pl.BlockSpec((None, n_heads, 1, dqk), lambda b, c: (b, 0, 0, 0)),
                   pl.BlockSpec((None, n_heads, 1, 1), lambda b, c: (b, 0, 0, 0))]
    if init is not None:
        c0, n0, m0 = init
        in_specs += state_specs
        args += [c0, n0.reshape(n_seq, n_heads, 1, dqk), m0.reshape(n_seq, n_heads, 1, 1)]
    y, c_new, n_new, m_new = pl.pallas_call(
        functools.partial(_mlstm_kernel, n_heads=n_heads, dqk=dqk, dv=dv, n_chunks=nc, has_init=init is not None),
        grid=(n_seq, nc), in_specs=in_specs,
        out_specs=[pl.BlockSpec((ln, wv), lambda b, c: (b * nc + c, 0))] + state_specs,
        out_shape=[jax.ShapeDtypeStruct((m, wv), out_dtype),
                   jax.ShapeDtypeStruct((n_seq, n_heads, dqk, dv), F32),
                   jax.ShapeDtypeStruct((n_seq, n_heads, 1, dqk), F32),
                   jax.ShapeDtypeStruct((n_seq, n_heads, 1, 1), F32)],
        scratch_shapes=[pltpu.VMEM((n_heads, dqk, dv), F32), pltpu.VMEM((n_heads, 1, dqk), F32),
                        pltpu.VMEM((n_heads, 1, 1), F32)],
        compiler_params=_params(2), name="mlstm",
    )(*args)
    return y, c_new, n_new.reshape(n_seq, n_heads, dqk), m_new.reshape(n_seq, n_heads)


def _mla_post_kernel(z_ref, z2_ref, gq_ref, gkv_ref, cos_ref, sin_ref, cq_ref, ckv_ref, kpe_ref, *, q_lora):
    z = z_ref[...]
    cq_ref[...] = (_rms(z[:, :q_lora]) * gq_ref[...]).astype(cq_ref.dtype)
    ckv_ref[...] = _rms(z[:, q_lora:]) * gkv_ref[...]
    z2 = z2_ref[...]
    r = z2.shape[1] // 2
    kpe_ref[...] = _pos_mul(z2[:, :r], cos_ref[...]) + _pos_mul(z2[:, r:], sin_ref[...])


def _mla_post(z, z2, g_q, g_kv, cos, sin, t):
    m, zc = z.shape
    q_lora, kv_lora, rope = g_q.shape[0], g_kv.shape[0], cos.shape[1]
    tm = _row_tile(m, t, NORM_ROW_TILE)
    return pl.pallas_call(
        functools.partial(_mla_post_kernel, q_lora=q_lora),
        grid=(m // tm, 1),
        in_specs=[pl.BlockSpec((tm, zc), lambda i, j: (i, 0)),
                  pl.BlockSpec((tm, 2 * rope), lambda i, j: (i, 0)),
                  pl.BlockSpec((1, q_lora), lambda i, j: (0, 0)),
                  pl.BlockSpec((1, kv_lora), lambda i, j: (0, 0)),
                  _pos_spec(cos, t, tm, rope), _pos_spec(sin, t, tm, rope)],
        out_specs=[pl.BlockSpec((tm, q_lora), lambda i, j: (i, 0)),
                   pl.BlockSpec((tm, kv_lora), lambda i, j: (i, 0)),
                   pl.BlockSpec((tm, rope), lambda i, j: (i, 0))],
        out_shape=[jax.ShapeDtypeStruct((m, q_lora), BF16),
                   jax.ShapeDtypeStruct((m, kv_lora), F32),
                   jax.ShapeDtypeStruct((m, rope), F32)],
        compiler_params=_params(2), name="mla_post",
    )(z, z2, g_q.reshape(1, q_lora), g_kv.reshape(1, kv_lora), cos, sin)


def _rope_linear_kernel(a_ref, w_ref, wrot_ref, cos_ref, sin_ref, o_ref):
    a = a_ref[...]
    x = _dot(a, w_ref[...].astype(BF16))
    x_rot = _dot(a, wrot_ref[...].astype(BF16))
    o_ref[...] = _pos_mul(x, cos_ref[...]) + _pos_mul(x_rot, sin_ref[...])


def _rope_linear(a, w, w_rot, cos, sin, t):
    m, kdim = a.shape
    n = w.shape[1]
    tm = _row_tile(m, t, ROW_TILE)
    tn = _tile(n, COL_TILE, 128)
    return pl.pallas_call(
        _rope_linear_kernel, grid=(m // tm, n // tn),
        in_specs=[pl.BlockSpec((tm, kdim), lambda i, j: (i, 0)),
                  pl.BlockSpec((kdim, tn), lambda i, j: (0, j)),
                  pl.BlockSpec((kdim, tn), lambda i, j: (0, j)),
                  _pos_spec(cos, t, tm, tn), _pos_spec(sin, t, tm, tn)],
        out_specs=pl.BlockSpec((tm, tn), lambda i, j: (i, j)),
        out_shape=jax.ShapeDtypeStruct((m, n), F32),
        compiler_params=_params(2), name="rope_linear",
    )(a, w, w_rot, cos, sin)


def _mla_prompt_kernel(qi_ref, kj_ref, last_ref, qn_ref, qp_ref, kv_ref, kpe_ref, o_ref, m_ref, l_ref, acc_ref,
                       *, hb, nope, rope, vd):
    pair = pl.program_id(2)
    qi, kj, last = qi_ref[pair], kj_ref[pair], last_ref[pair]
    tq, tk = qn_ref.shape[0], kv_ref.shape[0]
    scale = (nope + rope) ** -0.5 * LOG2E

    @pl.when(kj == 0)
    def _():
        _softmax_init(m_ref, l_ref, acc_ref)

    def run(masked):
        kpe = kpe_ref[...].astype(BF16)
        if masked:
            visible = _causal_visible(qi, kj, tq, tk, tq)
        for h in range(hb):
            q = jnp.concatenate([qn_ref[:, h * nope:(h + 1) * nope].astype(BF16),
                                 qp_ref[:, h * rope:(h + 1) * rope].astype(BF16)], axis=1)
            k = jnp.concatenate([kv_ref[:, h * (nope + vd): h * (nope + vd) + nope], kpe], axis=1)
            v = kv_ref[:, h * (nope + vd) + nope: (h + 1) * (nope + vd)]
            s = _dot_nt(q, k) * scale
            if masked:
                s = jnp.where(visible, s, NEG_INF)
            m_ref[h], l_ref[h], acc_ref[h] = _softmax_fold(s, v, m_ref[h], l_ref[h], acc_ref[h])

    @pl.when(last == 0)
    def _():
        run(False)

    @pl.when(last == 1)
    def _():
        run(True)
        for h in range(hb):
            o_ref[:, h * vd:(h + 1) * vd] = (acc_ref[h] / l_ref[h]).astype(o_ref.dtype)


def _mla_prompt(qn, qp, kv, kpe, n_seq, t, n_heads, nope, rope, vd):
    m = qn.shape[0]
    tq = _tile(t, MLA_TQ, BF16_ROWS)
    tk = _tile(t, MLA_TK, tq)
    nq, nk = t // tq, t // tk
    hb = _tile(n_heads, MLA_HEAD_BLOCK, 2)
    tabs = _causal_pairs(nq, tq, tk)
    grid_spec = pltpu.PrefetchScalarGridSpec(
        num_scalar_prefetch=3, grid=(n_seq, n_heads // hb, tabs[0].shape[0]),
        in_specs=[pl.BlockSpec((tq, hb * nope), lambda b, g, p, qi, kj, la: (b * nq + qi[p], g)),
                  pl.BlockSpec((tq, hb * rope), lambda b, g, p, qi, kj, la: (b * nq + qi[p], g)),
                  pl.BlockSpec((tk, hb * (nope + vd)), lambda b, g, p, qi, kj, la: (b * nk + kj[p], g)),
                  pl.BlockSpec((tk, rope), lambda b, g, p, qi, kj, la: (b * nk + kj[p], 0))],
        out_specs=pl.BlockSpec((tq, hb * vd), lambda b, g, p, qi, kj, la: (b * nq + qi[p], g)),
        scratch_shapes=[pltpu.VMEM((hb, tq, 1), F32), pltpu.VMEM((hb, tq, 1), F32),
                        pltpu.VMEM((hb, tq, vd), F32)])
    return pl.pallas_call(
        functools.partial(_mla_prompt_kernel, hb=hb, nope=nope, rope=rope, vd=vd),
        grid_spec=grid_spec, out_shape=jax.ShapeDtypeStruct((m, n_heads * vd), BF16),
        compiler_params=_params(3), name="mla_attn_prompt",
    )(*tabs, qn, qp, kv, kpe)


def _q_latent_kernel(qn_ref, wuk_ref, o_ref):
    x = _dot_nt(qn_ref[...].astype(BF16), wuk_ref[...].astype(BF16))
    o_ref[...] = x.reshape(o_ref.shape)


def _q_latent(qn, w_kvb2, n_seq, t, n_heads, nope, vd):
    m = qn.shape[0]
    kvl = w_kvb2.shape[0]
    assert nope == vd
    return pl.pallas_call(
        _q_latent_kernel, grid=(n_heads,),
        in_specs=[pl.BlockSpec((m, nope), lambda h: (0, h)),
                  pl.BlockSpec((kvl, nope), lambda h: (0, 2 * h))],
        out_specs=pl.BlockSpec((n_seq, None, t, kvl), lambda h: (0, h, 0, 0)),
        out_shape=jax.ShapeDtypeStruct((n_seq, n_heads, t, kvl), F32),
        compiler_params=_params(1), name="mla_q_latent",
    )(qn, w_kvb2)


def _v_up_kernel(ol_ref, wuv_ref, o_ref):
    n_seq, t, kvl = ol_ref.shape
    x = ol_ref[...].reshape(n_seq * t, kvl).astype(BF16)
    o_ref[...] = _dot(x, wuv_ref[...].astype(BF16))


def _v_up(o_lat, w_kvb2, nope, vd):
    n_seq, n_heads, t, kvl = o_lat.shape
    assert nope == vd
    return pl.pallas_call(
        _v_up_kernel, grid=(n_heads,),
        in_specs=[pl.BlockSpec((n_seq, None, t, kvl), lambda h: (0, h, 0, 0)),
                  pl.BlockSpec((kvl, vd), lambda h: (0, 2 * h + 1))],
        out_specs=pl.BlockSpec((n_seq * t, vd), lambda h: (0, h)),
        out_shape=jax.ShapeDtypeStruct((n_seq * t, n_heads * vd), F32),
        compiler_params=_params(1), name="mla_v_up",
    )(o_lat, w_kvb2)


def _mla_sample_kernel(pt_ref, ql_ref, qp_ref, cn_ref, rn_ref, *rest,
                       n_heads, rope, pages, chunk, n_steps, scale):
    del pt_ref
    cp_refs, rp_refs = rest[:pages], rest[pages:2 * pages]
    o_ref, m_ref, l_ref, acc_ref, ql_scr, qp_scr = rest[2 * pages:]
    g = pl.program_id(1)
    _, t, kvl = ql_ref.shape
    rows = n_heads * t

    @pl.when(g == 0)
    def _():
        _softmax_init(m_ref, l_ref, acc_ref)
        ql_scr[...] = ql_ref[...].reshape(rows, kvl).astype(BF16)
        qp = qp_ref[...]
        qp_scr[...] = jnp.concatenate([qp[:, h * rope:(h + 1) * rope] for h in range(n_heads)],
                                      axis=0).astype(BF16)

    q_lat, q_pe = ql_scr[...], qp_scr[...]
    parts = []
    for c0 in range(0, pages, chunk):
        ckv = jnp.concatenate([r[...] for r in cp_refs[c0:c0 + chunk]], axis=0).astype(BF16)
        kpe_t = jnp.concatenate([r[...] for r in rp_refs[c0:c0 + chunk]], axis=1).astype(BF16)
        s = (_dot_nt(q_lat, ckv) + _dot(q_pe, kpe_t)) * scale
        parts.append(_softmax_local(s, ckv))
    state = _softmax_merge((m_ref[0], l_ref[0], acc_ref[0]), parts)

    @pl.when(g < n_steps - 1)
    def _():
        m_ref[0], l_ref[0], acc_ref[0] = state

    @pl.when(g == n_steps - 1)
    def _():
        page = cp_refs[0].shape[0]
        cn = _pad_keys(cn_ref[...], page).astype(BF16)
        rn = _pad_keys(rn_ref[...], page).astype(BF16)
        sn = (_dot_nt(q_lat, cn) + _dot_nt(q_pe, rn)) * scale
        _, l_fin, acc_fin = _softmax_fold(jnp.where(_new_key_mask(rows, t, page), sn, NEG_INF), cn, *state)
        o_ref[...] = (acc_fin / l_fin).reshape(o_ref.shape)


def _mla_sample(q_lat, qp, ckv_new, kpe_new, cache_ckv, cache_kpe_t, layer_slot, page_table, nope):
    n_seq, n_heads, t, kvl = q_lat.shape
    rope = kpe_new.shape[1]
    n_pages = page_table.shape[1]
    page = cache_ckv.shape[2]
    pages = _tile(n_pages, MLA_PAGES_PER_STEP, 1)
    chunk = _tile(pages, PAGES_PER_CHUNK, 1)
    n_steps = n_pages // pages
    rows = n_heads * t

    def page_spec(blk, p):
        return pl.BlockSpec((None, None) + blk, lambda s, g, pt: (layer_slot, pt[s, g * pages + p], 0, 0))

    grid_spec = pltpu.PrefetchScalarGridSpec(
        num_scalar_prefetch=1, grid=(n_seq, n_steps),
        in_specs=[pl.BlockSpec((None, n_heads, t, kvl), lambda s, g, pt: (s, 0, 0, 0)),
                  pl.BlockSpec((t, n_heads * rope), lambda s, g, pt: (s, 0)),
                  pl.BlockSpec((t, kvl), lambda s, g, pt: (s, 0)),
                  pl.BlockSpec((t, rope), lambda s, g, pt: (s, 0))]
                 + [page_spec((page, kvl), p) for p in range(pages)]
                 + [page_spec((rope, page), p) for p in range(pages)],
        out_specs=pl.BlockSpec((None, n_heads, t, kvl), lambda s, g, pt: (s, 0, 0, 0)),
        scratch_shapes=[pltpu.VMEM((1, rows, 1), F32), pltpu.VMEM((1, rows, 1), F32),
                        pltpu.VMEM((1, rows, kvl), F32),
                        pltpu.VMEM((rows, kvl), BF16), pltpu.VMEM((rows, rope), BF16)])
    return pl.pallas_call(
        functools.partial(_mla_sample_kernel, n_heads=n_heads, rope=rope, pages=pages, chunk=chunk,
                          n_steps=n_steps, scale=(nope + rope) ** -0.5 * LOG2E),
        grid_spec=grid_spec, out_shape=jax.ShapeDtypeStruct((n_seq, n_heads, t, kvl), F32),
        compiler_params=_params(2), name="mla_attn_sample",
    )(page_table, q_lat, qp, ckv_new, kpe_new, *([cache_ckv] * pages), *([cache_kpe_t] * pages))


def _rotate_half_cols(w, rope):
    k, n = w.shape
    w3 = w.reshape(k, n // rope, rope)
    half = rope // 2
    return jnp.concatenate([-w3[..., half:], w3[..., :half]], axis=-1).reshape(k, n)


def _rope_tables(pos, rope, repeat):
    half = rope // 2
    freqs = ROPE_THETA ** (-jnp.arange(half, dtype=F32) / half)
    ang = pos.astype(F32)[:, None] * freqs[None, :]
    cos = jnp.tile(jnp.cos(ang), (1, 2 * repeat))
    sin = jnp.tile(jnp.sin(ang), (1, 2 * repeat))
    return cos, sin


def _diff_lambda_init(layer):
    return 0.8 - 0.6 * math.exp(-0.3 * layer)


def kernel(x_prompt, x_sample, cache_k_a, cache_v_a, state_C_b, state_n_b, state_m_b, cache_ckv_c, cache_kpe_c,
           state_ffn_conv, page_table, c_prompt, c_sample, w_ada, b_ada, ada_table, g_mix, g_ffn, g_final,
           a_w_qkv, a_lambda, a_g_sub, a_w_o, b_w_in, b_gate_bias, b_g_out, b_w_out,
           c_w_in, c_g_q, c_g_kv, c_w_qb, c_w_kvb, c_w_o, f_w_up, f_conv_w, f_conv_b, f_w_down):
    nb, tp, d = x_prompt.shape
    ns, ts, _ = x_sample.shape
    depth = g_mix.shape[0]
    page = cache_k_a.shape[2]
    past_len = page_table.shape[1] * page
    assert ts == SUBLANES and f_conv_w.shape[1] == CONV_W

    dh = a_lambda.shape[-1]
    a_heads = a_w_o.shape[1] // (2 * dh)
    b_heads, dqk, dv = state_C_b.shape[2:]
    kvl, rope, q_lora = cache_ckv_c.shape[-1], cache_kpe_c.shape[-1], c_g_q.shape[-1]
    c_heads = c_w_kvb.shape[2]
    nope = c_w_qb.shape[2] // c_heads - rope
    vd = c_w_kvb.shape[3] - nope
    d_ff = f_conv_w.shape[-1]
    f_pad = -(-d_ff // FFN_PAD) * FFN_PAD

    groups = (("p", nb, tp), ("s", ns, ts))
    x = {"p": x_prompt.reshape(nb * tp, d), "s": x_sample.reshape(ns * ts, d)}

    pad = -nb % SUBLANES
    c_all = jnp.concatenate([c_sample, c_prompt, jnp.zeros((pad, d), F32)], axis=0)
    mod_s, mod_p = _ada(c_all, w_ada, b_ada, ada_table, ns)
    mods = {"p": mod_p[:, :nb].reshape(depth, nb, 1, -1), "s": mod_s}

    pos = {"p": jnp.arange(tp, dtype=jnp.int32), "s": past_len + jnp.arange(ts, dtype=jnp.int32)}
    cache_k = cache_k_a.reshape(cache_k_a.shape[:2] + (2 * page, dh))
    cache_v = cache_v_a.reshape(cache_v_a.shape[:2] + (2 * page, dh))
    cache_kpe_t = jnp.swapaxes(cache_kpe_c, 2, 3)

    outs = {key: {"p": [], "s": []} for key in ("ka", "va", "cb", "nb", "mb", "ckv", "kpe", "cv")}

    for l in range(depth):
        kind, slot = l % N_MIXERS, l // N_MIXERS
        w_down = _cast_pad_rows(f_w_down, l, f_pad)
        if kind == 1:
            n_main = 2 * b_heads * (dqk + dv)
            w_gate = jnp.pad(b_w_in[slot][:, n_main:], ((0, 0), (0, 128 - 2 * b_heads)))
        elif kind == 2:
            w_kpe = c_w_in[slot][:, q_lora + kvl:]
            w_kpe2 = jnp.concatenate([w_kpe, _rotate_half_cols(w_kpe, rope)], axis=1)
            wq3 = c_w_qb[slot].reshape(q_lora, c_heads, nope + rope)
            w_qn = wq3[:, :, :nope].reshape(q_lora, c_heads * nope)
            w_qr = wq3[:, :, nope:].reshape(q_lora, c_heads * rope)
            w_qr_rot = _rotate_half_cols(w_qr, rope)
            w_kvb2 = c_w_kvb[slot].reshape(kvl, c_heads * (nope + vd))

        for key, n_seq, t in groups:
            mod = mods[key]
            xg = x[key]
            h = _norm(xg, g_mix, l, t, mod, 0, 1)
            if kind == 0:
                lam_init = _diff_lambda_init(l)
                z = _linear(h, a_w_qkv, t, layer=slot)
                qw = a_heads * 2 * dh
                k_new, v_new = z[:, qw:qw + 2 * dh], z[:, qw + 2 * dh:]
                outs["ka"][key].append(k_new.reshape(n_seq, t, 1, 2 * dh))
                outs["va"][key].append(v_new.reshape(n_seq, t, 1, 2 * dh))
                if key == "p":
                    o = _diff_prompt(z, a_lambda[slot], a_g_sub[slot], n_seq, t, a_heads, dh, lam_init)
                else:
                    o = _diff_sample(z, cache_k, cache_v, slot, page_table, a_lambda[slot], a_g_sub[slot],
                                     t, a_heads, dh, lam_init).astype(BF16)
                xg = _linear(o, a_w_o, t, layer=slot, residual=(xg, mod, l, 2))
            elif kind == 1:
                z = _linear(h, b_w_in, t, layer=slot, n_out=n_main)
                zg = _linear(h, w_gate, t)
                if key == "p":
                    y, c_new, n_new, m_new = _mlstm(z, zg, b_gate_bias[slot], b_g_out[slot], n_seq, t,
                                                    b_heads, dqk, dv)
                else:
                    y, c_new, n_new, m_new = _mlstm(z, zg, b_gate_bias[slot], b_g_out[slot], n_seq, t,
                                                    b_heads, dqk, dv, out_dtype=F32,
                                                    init=(state_C_b[slot], state_n_b[slot], state_m_b[slot]))
                    y = y.astype(BF16)
                outs["cb"][key].append(c_new)
                outs["nb"][key].append(n_new)
                outs["mb"][key].append(m_new)
                xg = _linear(y, b_w_out, t, layer=slot, residual=(xg, mod, l, 2))
            else:
                z = _linear(h, c_w_in, t, layer=slot, n_out=q_lora + kvl)
                z2 = _linear(h, w_kpe2, t)
                cos1, sin1 = _rope_tables(pos[key], rope, 1)
                cq, ckv, kpe = _mla_post(z, z2, c_g_q[slot], c_g_kv[slot], cos1, sin1, t)
                outs["ckv"][key].append(ckv.reshape(n_seq, t, kvl))
                outs["kpe"][key].append(kpe.reshape(n_seq, t, rope))
                cosh, sinh = _rope_tables(pos[key], rope, c_heads)
                qn = _linear(cq, w_qn, t, tn=DOWN_COL_TILE)
                qp = _rope_linear(cq, w_qr, w_qr_rot, cosh, sinh, t)
                if key == "p":
                    kv = _linear(ckv, w_kvb2, t, tn=KV_UP_COL_TILE, out_dtype=BF16)
                    o = _mla_prompt(qn, qp, kv, kpe, n_seq, t, c_heads, nope, rope, vd)
                else:
                    q_lat = _q_latent(qn, w_kvb2, n_seq, t, c_heads, nope, vd)
                    o_lat = _mla_sample(q_lat, qp, ckv, kpe, cache_ckv_c, cache_kpe_t, slot, page_table, nope)
                    o = _v_up(o_lat, w_kvb2, nope, vd).astype(BF16)
                xg = _linear(o, c_w_o, t, layer=slot, k_steps=2, residual=(xg, mod, l, 2))

            h = _norm(xg, g_ffn, l, t, mod, 3, 4)
            prev = state_ffn_conv[l] if key == "s" else None
            g, tail = _ffn_up(h, f_w_up, f_conv_w, f_conv_b, l, t, prev)
            outs["cv"][key].append(tail.reshape(n_seq, SUBLANES, -1)[:, SUBLANES - (CONV_W - 1):, :d_ff])
            x[key] = _linear(g, w_down, t, tn=DOWN_COL_TILE, k_steps=FFN_DOWN_K_STEPS,
                             residual=(xg, mod, l, 5))

    y_p = _norm(x["p"], g_final, 0, tp, out_dtype=F32).reshape(nb, tp, d)
    y_s = _norm(x["s"], g_final, 0, ts, out_dtype=F32).reshape(ns, ts, d)
    res = [y_p, y_s]
    for key in ("ka", "va", "cb", "nb", "mb", "ckv", "kpe", "cv"):
        res += [jnp.stack(outs[key]["p"]), jnp.stack(outs[key]["s"])]
    return tuple(res)
```

```python
import functools
import math

import jax
import jax.numpy as jnp
from jax import lax
from jax.experimental import pallas as pl
from jax.experimental.pallas import tpu as pltpu

F32 = jnp.float32
BF16 = jnp.bfloat16

EPS = 1e-6
NEG_INF = -1e30
LOG2E = math.log2(math.e)
GATE_CAP = 15.0
ROPE_THETA = 10000.0
N_MIXERS = 3
CONV_W = 3

V7X_VMEM_LIMIT_BYTES = 56 << 20
SUBLANES = 8
BF16_ROWS = 16

ROW_TILE = 1024
NORM_ROW_TILE = 512
COL_TILE = 512
DOWN_COL_TILE = 1024
KV_UP_COL_TILE = 2048
FFN_COL_TILE = 256
FFN_PAD = 1024
FFN_DOWN_K_STEPS = 4
CAST_ROW_TILE = 256
MLSTM_CHUNK = 128
DIFF_TQ = 256
DIFF_TK = 1024
DIFF_HEADS_PER_ITER = 1
DIFF_UNROLL = 2
MLA_TQ = 256
MLA_TK = 1024
MLA_HEAD_BLOCK = 4
PAGES_PER_STEP = 16
MLA_PAGES_PER_STEP = 16
PAGES_PER_CHUNK = 16


def _tile(n, pref, align):
    if n <= pref:
        return n
    t = (pref // align) * align
    while t >= align:
        if n % t == 0:
            return t
        t -= align
    raise ValueError(f"no tile for {n} (pref {pref}, align {align})")


def _params(n_axes):
    return pltpu.CompilerParams(dimension_semantics=("arbitrary",) * n_axes,
                                vmem_limit_bytes=V7X_VMEM_LIMIT_BYTES)


def _dot(a, b):
    return jnp.dot(a, b, preferred_element_type=F32)


def _dot_nt(a, b):
    return lax.dot_general(a, b, (((1,), (1,)), ((), ())), preferred_element_type=F32)


def _dot_tn(a, b):
    return lax.dot_general(a, b, (((0,), (0,)), ((), ())), preferred_element_type=F32)


def _w_spec(w, layer, blk, idx):
    if w.ndim == len(blk) + 1:
        return pl.BlockSpec((None,) + tuple(blk), lambda *g: (layer,) + tuple(idx(*g)))
    return pl.BlockSpec(tuple(blk), idx)


def _row_tile(m, t, pref):
    if t >= BF16_ROWS:
        return _tile(t, pref, BF16_ROWS)
    return _tile(m, pref, BF16_ROWS)


def _vec_spec(vec, layer, t, tm, tn, col_blk):
    if vec.ndim == 4:
        assert t % tm == 0
        return pl.BlockSpec((None, 1, 1, tn), lambda i, *g: (layer, i // (t // tm), 0, col_blk(*g)))
    assert tm % t == 0
    return pl.BlockSpec((None, tm // t, tn), lambda i, *g: (layer, i, col_blk(*g)))


def _vec(ref):
    return ref[0] if len(ref.shape) == 3 else ref[...]


def _seq_apply(x, v, fn):
    g = v.shape[0]
    if g == 1:
        return fn(x, v)
    tm, n = x.shape
    return fn(x.reshape(g, tm // g, n), v[:, None, :]).reshape(tm, n)


def _pos_mul(x, tab):
    if tab.shape[0] == x.shape[0]:
        return x * tab
    tm, n = x.shape
    return (x.reshape(tm // tab.shape[0], tab.shape[0], n) * tab[None]).reshape(tm, n)


def _pos_spec(tab, t, tm, tn):
    if t >= tm:
        return pl.BlockSpec((tm, tn), lambda i, j, *g: (i % (t // tm), j))
    return pl.BlockSpec((t, tn), lambda i, j, *g: (0, j))


def _rms(x):
    return x * lax.rsqrt(jnp.mean(x * x, axis=-1, keepdims=True) + EPS)


def _ada_kernel(c_ref, w_ref, b_ref, tab_ref, os_ref, op_ref, *, depth, n_sample):
    c = c_ref[...]
    a = (c * jax.nn.sigmoid(c)).astype(BF16)
    base = _dot(a, w_ref[...].astype(BF16)) + b_ref[...]
    for l in range(depth):
        full = base + tab_ref[l]
        os_ref[l] = full[:n_sample]
        op_ref[l] = full[n_sample:]


def _ada(c_all, w_ada, b_ada, ada_table, n_sample):
    r, d = c_all.shape
    n = w_ada.shape[1]
    depth = ada_table.shape[0]
    tn = _tile(n, COL_TILE, 128)
    return pl.pallas_call(
        functools.partial(_ada_kernel, depth=depth, n_sample=n_sample),
        grid=(n // tn,),
        in_specs=[pl.BlockSpec((r, d), lambda j: (0, 0)),
                  pl.BlockSpec((d, tn), lambda j: (0, j)),
                  pl.BlockSpec((1, tn), lambda j: (0, j)),
                  pl.BlockSpec((depth, 1, tn), lambda j: (0, 0, j))],
        out_specs=[pl.BlockSpec((depth, n_sample, tn), lambda j: (0, 0, j)),
                   pl.BlockSpec((depth, r - n_sample, tn), lambda j: (0, 0, j))],
        out_shape=[jax.ShapeDtypeStruct((depth, n_sample, n), F32),
                   jax.ShapeDtypeStruct((depth, r - n_sample, n), F32)],
        compiler_params=_params(1), name="ada",
    )(c_all, w_ada, b_ada.reshape(1, n), ada_table.reshape(depth, 1, n))


def _norm_kernel(*refs, modulated):
    if modulated:
        x_ref, g_ref, sh_ref, sc_ref, o_ref = refs
    else:
        x_ref, g_ref, o_ref = refs
    y = _rms(x_ref[...]) * g_ref[...]
    if modulated:
        y = _seq_apply(y, _vec(sc_ref), lambda a, s: a * (1.0 + s))
        y = _seq_apply(y, _vec(sh_ref), lambda a, s: a + s)
    o_ref[...] = y.astype(o_ref.dtype)


def _norm(x, g, layer, t, mod=None, shift_chunk=0, scale_chunk=1, out_dtype=BF16):
    m, d = x.shape
    tm = _row_tile(m, t, NORM_ROW_TILE)
    g = g.reshape(g.shape[:-1] + (1, d))
    in_specs = [pl.BlockSpec((tm, d), lambda i: (i, 0)), _w_spec(g, layer, (1, d), lambda i: (0, 0))]
    args = [x, g]
    if mod is not None:
        in_specs += [_vec_spec(mod, layer, t, tm, d, lambda: shift_chunk),
                     _vec_spec(mod, layer, t, tm, d, lambda: scale_chunk)]
        args += [mod, mod]
    return pl.pallas_call(
        functools.partial(_norm_kernel, modulated=mod is not None),
        grid=(m // tm,), in_specs=in_specs,
        out_specs=pl.BlockSpec((tm, d), lambda i: (i, 0)),
        out_shape=jax.ShapeDtypeStruct((m, d), out_dtype),
        compiler_params=_params(1), name="norm",
    )(*args)


def _linear_kernel(*refs, nk, residual):
    if residual:
        a_ref, w_ref, x_ref, gate_ref, o_ref, *scratch = refs
    else:
        a_ref, w_ref, o_ref, *scratch = refs
    part = _dot(a_ref[...].astype(BF16), w_ref[...].astype(BF16))

    def finish(acc):
        if residual:
            acc = x_ref[...] + _seq_apply(acc, _vec(gate_ref), lambda a, s: s * a)
        o_ref[...] = acc.astype(o_ref.dtype)

    if nk == 1:
        finish(part)
        return
    (acc_ref,) = scratch
    k = pl.program_id(2)

    @pl.when(k == 0)
    def _():
        acc_ref[...] = part

    @pl.when(k > 0)
    def _():
        acc_ref[...] += part

    @pl.when(k == nk - 1)
    def _():
        finish(acc_ref[...])


def _linear(a, w, t, *, layer=0, n_out=None, col0=0, tn=COL_TILE, k_steps=1, out_dtype=F32, residual=None):
    m, kdim = a.shape
    n_out = n_out or w.shape[-1]
    tm = _row_tile(m, t, ROW_TILE)
    tn = _tile(n_out, tn, 128)
    assert col0 % tn == 0 and kdim % k_steps == 0 and w.shape[-2] == kdim
    tk = kdim // k_steps
    cb = col0 // tn
    in_specs = [pl.BlockSpec((tm, tk), lambda i, j, k: (i, k)),
                _w_spec(w, layer, (tk, tn), lambda i, j, k: (k, cb + j))]
    args = [a, w]
    if residual is not None:
        x, gate, mod_layer, chunk = residual
        assert x.shape == (m, n_out)
        in_specs += [pl.BlockSpec((tm, tn), lambda i, j, k: (i, j)),
                     _vec_spec(gate, mod_layer, t, tm, tn, lambda j, k: chunk * (n_out // tn) + j)]
        args += [x, gate]
    return pl.pallas_call(
        functools.partial(_linear_kernel, nk=k_steps, residual=residual is not None),
        grid=(m // tm, n_out // tn, k_steps), in_specs=in_specs,
        out_specs=pl.BlockSpec((tm, tn), lambda i, j, k: (i, j)),
        out_shape=jax.ShapeDtypeStruct((m, n_out), out_dtype),
        scratch_shapes=[pltpu.VMEM((tm, tn), F32)] if k_steps > 1 else [],
        compiler_params=_params(3), name="linear",
    )(*args)


def _cast_pad_kernel(w_ref, o_ref, *, n_valid):
    i = pl.program_id(0)
    o_ref[...] = jnp.where(i < n_valid, w_ref[...], 0.0).astype(o_ref.dtype)


def _cast_pad_rows(w, layer, rows_pad):
    _, k, n = w.shape
    tr = _tile(math.gcd(k, rows_pad), CAST_ROW_TILE, BF16_ROWS)
    n_valid = k // tr
    return pl.pallas_call(
        functools.partial(_cast_pad_kernel, n_valid=n_valid),
        grid=(rows_pad // tr,),
        in_specs=[pl.BlockSpec((None, tr, n), lambda i: (layer, jnp.minimum(i, n_valid - 1), 0))],
        out_specs=pl.BlockSpec((tr, n), lambda i: (i, 0)),
        out_shape=jax.ShapeDtypeStruct((rows_pad, n), BF16),
        compiler_params=_params(1), name="cast_pad",
    )(w)


def _ffn_up_kernel(*refs, t, n_valid, has_prev):
    if has_prev:
        h_ref, wa_ref, wu_ref, cw_ref, cb_ref, e1_ref, e2_ref, g_ref, tail_ref = refs
    else:
        h_ref, wa_ref, wu_ref, cw_ref, cb_ref, g_ref, tail_ref = refs
    j = pl.program_id(1)
    h = h_ref[...]
    a = _dot(h, wa_ref[...].astype(BF16))
    u = _dot(h, wu_ref[...].astype(BF16))
    tm = a.shape[0]
    r = lax.broadcasted_iota(jnp.int32, a.shape, 0)
    if tm != t:
        r = lax.rem(r, t)
    back1 = pltpu.roll(a, 1, 0)
    back2 = pltpu.roll(a, 2, 0)
    back1 = jnp.where(r >= 1, back1, e1_ref[...] if has_prev else 0.0)
    back2 = jnp.where(r >= 2, back2, e2_ref[...] if has_prev else 0.0)
    cw = cw_ref[...]
    ac = cb_ref[...] + cw[0:1] * back2 + cw[1:2] * back1 + cw[2:3] * a
    g = ac * jax.nn.sigmoid(ac) * u
    g_ref[...] = jnp.where(j < n_valid, g, 0.0).astype(g_ref.dtype)
    tail_ref[...] = a[tm - SUBLANES:, :] if t == tm else a


def _ffn_up(h, w_up, conv_w, conv_b, layer, t, prev=None):
    m, d = h.shape
    f = conv_w.shape[-1]
    tm = t if t >= BF16_ROWS else m
    assert t == tm or t == SUBLANES
    tn = _tile(f, FFN_COL_TILE, 128)
    n_valid = f // tn
    f_pad = -(-f // FFN_PAD) * FFN_PAD
    n_tiles = f_pad // tn
    tail_rows = SUBLANES if t == tm else tm

    def col(j):
        return jnp.minimum(j, n_valid - 1)

    conv_b = conv_b.reshape(conv_b.shape[0], 1, f)
    in_specs = [pl.BlockSpec((tm, d), lambda i, j: (i, 0), pipeline_mode=pl.Buffered(1)),
                _w_spec(w_up, layer, (d, tn), lambda i, j: (0, col(j))),
                _w_spec(w_up, layer, (d, tn), lambda i, j: (0, n_valid + col(j))),
                _w_spec(conv_w, layer, (CONV_W, tn), lambda i, j: (0, col(j))),
                _w_spec(conv_b, layer, (1, tn), lambda i, j: (0, col(j)))]
    args = [h, w_up, w_up, conv_w, conv_b]
    if prev is not None:
        n_seq = m // t
        zeros = jnp.zeros((n_seq, t - 2, f), F32)
        e2 = jnp.concatenate([prev, zeros], axis=1).reshape(m, f)
        e1 = jnp.concatenate([prev[:, 1:2], jnp.zeros((n_seq, 1, f), F32), zeros], axis=1).reshape(m, f)
        in_specs += [pl.BlockSpec((tm, tn), lambda i, j: (i, col(j)))] * 2
        args += [e1, e2]
    return pl.pallas_call(
        functools.partial(_ffn_up_kernel, t=t, n_valid=n_valid, has_prev=prev is not None),
        grid=(m // tm, n_tiles), in_specs=in_specs,
        out_specs=[pl.BlockSpec((tm, tn), lambda i, j: (i, j)),
                   pl.BlockSpec((tail_rows, tn), lambda i, j: (i, j))],
        out_shape=[jax.ShapeDtypeStruct((m, f_pad), BF16),
                   jax.ShapeDtypeStruct((m // tm * tail_rows, f_pad), F32)],
        compiler_params=_params(2), name="ffn_up",
    )(*args)


def _softmax_fold(s, v, m_old, l_old, acc_old):
    m_new = jnp.maximum(m_old, jnp.max(s, axis=1, keepdims=True))
    p = jnp.exp2(s - m_new)
    alpha = jnp.exp2(m_old - m_new)
    l_new = alpha * l_old + jnp.sum(p, axis=1, keepdims=True)
    acc_new = alpha * acc_old + _dot(p.astype(BF16), v)
    return m_new, l_new, acc_new


def _softmax_local(s, v):
    m = jnp.max(s, axis=1, keepdims=True)
    p = jnp.exp2(s - m)
    return m, jnp.sum(p, axis=1, keepdims=True), _dot(p.astype(BF16), v)


def _softmax_merge(state, parts):
    m_old, l_old, acc_old = state
    m_new = m_old
    for m, _, _ in parts:
        m_new = jnp.maximum(m_new, m)
    alpha = jnp.exp2(m_old - m_new)
    l_new, acc_new = alpha * l_old, alpha * acc_old
    for m, l, acc in parts:
        w = jnp.exp2(m - m_new)
        l_new = l_new + w * l
        acc_new = acc_new + w * acc
    return m_new, l_new, acc_new


def _softmax_init(m_ref, l_ref, acc_ref):
    m_ref[...] = jnp.full(m_ref.shape, NEG_INF, F32)
    l_ref[...] = jnp.zeros(l_ref.shape, F32)
    acc_ref[...] = jnp.zeros(acc_ref.shape, F32)


def _causal_pairs(nq, tq, tk):
    qi, kj, last = [], [], []
    for q in range(nq):
        n_kv = -(-(q + 1) * tq // tk)
        for k in range(n_kv):
            qi.append(q)
            kj.append(k)
            last.append(int(k == n_kv - 1))
    return tuple(jnp.asarray(a, jnp.int32) for a in (qi, kj, last))


def _causal_visible(qi, kj, tq, tk, rows, width):
    row = lax.rem(lax.broadcasted_iota(jnp.int32, (rows, width), 0), tq)
    col = lax.broadcasted_iota(jnp.int32, (rows, width), 1)
    return kj * tk + col <= qi * tq + row


def _for_each_causal_prefix(qi, tq, tk, fn):
    r = tk // tq
    for j in range(r):
        @pl.when(lax.rem(qi, r) == j)
        def _(j=j):
            fn((j + 1) * tq)


def _stack_heads(zq, n_heads, width, offset):
    return jnp.concatenate(
        [zq[:, h * 2 * width + offset: h * 2 * width + offset + width] for h in range(n_heads)], axis=0)


def _diff_lambda(lam_ref, lam_init):
    lp = lam_ref[...]
    return (jnp.exp(jnp.sum(lp[0:1] * lp[1:2], axis=1, keepdims=True))
            - jnp.exp(jnp.sum(lp[2:3] * lp[3:4], axis=1, keepdims=True)) + lam_init)


def _diff_finish(lam_ref, gsub_ref, o_ref, m_ref, l_ref, acc_ref, n_heads, lam_init):
    lam = _diff_lambda(lam_ref, lam_init)
    tq = acc_ref.shape[1] // n_heads
    w = acc_ref.shape[2]
    for h in range(n_heads):
        rows = slice(h * tq, (h + 1) * tq)
        o = acc_ref[0, rows, :] / l_ref[0, rows, :] - lam * (acc_ref[1, rows, :] / l_ref[1, rows, :])
        y = _rms(o) * gsub_ref[...] * (1.0 - lam_init)
        o_ref[:, h * w:(h + 1) * w] = y.astype(o_ref.dtype)


def _diff_prompt_kernel(qi_ref, kj_ref, last_ref, lam_ref, gsub_ref, zq_ref, k_ref, v_ref, o_ref,
                        m_ref, l_ref, acc_ref, q_scr, k_scr, v_scr, *, n_heads, dh, lam_init, hpi, unroll):
    pair = pl.program_id(1)
    qi, kj, last = qi_ref[pair], kj_ref[pair], last_ref[pair]
    tq, tk = zq_ref.shape[0], k_ref.shape[0]

    @pl.when(kj == 0)
    def _():
        _softmax_init(m_ref, l_ref, acc_ref)
        zq = zq_ref[...]
        for c in range(2):
            q_scr[c] = _stack_heads(zq, n_heads, dh, c * dh).astype(BF16)

    k_scr[...] = k_ref[...].astype(BF16)
    v_scr[...] = v_ref[...].astype(BF16)

    def run(n_keys, masked):
        rc = hpi * tq
        if masked:
            visible = _causal_visible(qi, kj, tq, tk, rc, n_keys)

        def body(i, carry):
            rows = pl.ds(pl.multiple_of(i * rc, rc), rc)
            v = v_scr[:n_keys, :]
            for c in range(2):
                s = _dot_nt(q_scr[c, rows, :], k_scr[:n_keys, c * dh:(c + 1) * dh]) * (dh ** -0.5 * LOG2E)
                if masked:
                    s = jnp.where(visible, s, NEG_INF)
                m_new, l_new, acc_new = _softmax_fold(s, v, m_ref[c, rows, :], l_ref[c, rows, :],
                                                      acc_ref[c, rows, :])
                m_ref[c, rows, :] = m_new
                l_ref[c, rows, :] = l_new
                acc_ref[c, rows, :] = acc_new
            return carry

        lax.fori_loop(0, n_heads // hpi, body, 0, unroll=unroll)

    @pl.when(last == 0)
    def _():
        run(tk, False)

    @pl.when(last == 1)
    def _():
        _for_each_causal_prefix(qi, tq, tk, lambda n_keys: run(n_keys, True))
        _diff_finish(lam_ref, gsub_ref, o_ref, m_ref, l_ref, acc_ref, n_heads, lam_init)


def _diff_prompt(z, lam_p, g_sub, n_seq, t, n_heads, dh, lam_init):
    m = z.shape[0]
    qw = n_heads * 2 * dh
    tq = _tile(t, DIFF_TQ, BF16_ROWS)
    tk = _tile(t, DIFF_TK, tq)
    nq, nk = t // tq, t // tk
    kcol = qw // (2 * dh)
    rows = n_heads * tq
    tabs = _causal_pairs(nq, tq, tk)
    grid_spec = pltpu.PrefetchScalarGridSpec(
        num_scalar_prefetch=3, grid=(n_seq, tabs[0].shape[0]),
        in_specs=[pl.BlockSpec((4, dh), lambda b, p, qi, kj, la: (0, 0)),
                  pl.BlockSpec((1, 2 * dh), lambda b, p, qi, kj, la: (0, 0)),
                  pl.BlockSpec((tq, qw), lambda b, p, qi, kj, la: (b * nq + qi[p], 0)),
                  pl.BlockSpec((tk, 2 * dh), lambda b, p, qi, kj, la: (b * nk + kj[p], kcol)),
                  pl.BlockSpec((tk, 2 * dh), lambda b, p, qi, kj, la: (b * nk + kj[p], kcol + 1))],
        out_specs=pl.BlockSpec((tq, qw), lambda b, p, qi, kj, la: (b * nq + qi[p], 0)),
        scratch_shapes=[pltpu.VMEM((2, rows, 1), F32), pltpu.VMEM((2, rows, 1), F32),
                        pltpu.VMEM((2, rows, 2 * dh), F32), pltpu.VMEM((2, rows, dh), BF16),
                        pltpu.VMEM((tk, 2 * dh), BF16), pltpu.VMEM((tk, 2 * dh), BF16)])
    return pl.pallas_call(
        functools.partial(_diff_prompt_kernel, n_heads=n_heads, dh=dh, lam_init=lam_init,
                          hpi=_tile(n_heads, DIFF_HEADS_PER_ITER, 1), unroll=DIFF_UNROLL),
        grid_spec=grid_spec, out_shape=jax.ShapeDtypeStruct((m, qw), BF16),
        compiler_params=_params(2), name="diff_attn_prompt",
    )(*tabs, lam_p, g_sub.reshape(1, 2 * dh), z, z, z)


def _pad_keys(x, n):
    return jnp.concatenate([x, jnp.zeros((n - x.shape[0], x.shape[1]), x.dtype)], axis=0)


def _new_key_mask(rows, t, n_keys):
    tok = lax.rem(lax.broadcasted_iota(jnp.int32, (rows, n_keys), 0), t)
    key = lax.broadcasted_iota(jnp.int32, (rows, n_keys), 1)
    return key <= tok


def _diff_sample_kernel(pt_ref, lam_ref, gsub_ref, zq_ref, kn_ref, vn_ref, *rest,
                        n_heads, dh, lam_init, pages, chunk, n_steps):
    del pt_ref
    kp_refs, vp_refs = rest[:pages], rest[pages:2 * pages]
    o_ref, m_ref, l_ref, acc_ref, q_scr = rest[2 * pages:]
    g = pl.program_id(1)
    t = zq_ref.shape[0]
    page = kp_refs[0].shape[0] // 2
    scale = dh ** -0.5 * LOG2E

    @pl.when(g == 0)
    def _():
        _softmax_init(m_ref, l_ref, acc_ref)
        zq = zq_ref[...]
        for c in range(2):
            q_scr[c] = _stack_heads(zq, n_heads, dh, c * dh).astype(BF16)

    def half(ref, c):
        return ref[pl.ds(c, page, stride=2), :]

    qs = [q_scr[0], q_scr[1]]
    parts = [[], []]
    for c0 in range(0, pages, chunk):
        v = jnp.concatenate([jnp.concatenate([half(r, 0), half(r, 1)], axis=1)
                             for r in vp_refs[c0:c0 + chunk]], axis=0).astype(BF16)
        for c in range(2):
            k = jnp.concatenate([half(r, c) for r in kp_refs[c0:c0 + chunk]], axis=0).astype(BF16)
            parts[c].append(_softmax_local(_dot_nt(qs[c], k) * scale, v))
    state = [_softmax_merge((m_ref[c], l_ref[c], acc_ref[c]), parts[c]) for c in range(2)]

    @pl.when(g < n_steps - 1)
    def _():
        for c in range(2):
            m_ref[c], l_ref[c], acc_ref[c] = state[c]

    @pl.when(g == n_steps - 1)
    def _():
        kn = _pad_keys(kn_ref[...], page).astype(BF16)
        vn = _pad_keys(vn_ref[...], page).astype(BF16)
        visible = _new_key_mask(n_heads * t, t, page)
        for c in range(2):
            s = _dot_nt(qs[c], kn[:, c * dh:(c + 1) * dh]) * scale
            m_ref[c], l_ref[c], acc_ref[c] = _softmax_fold(jnp.where(visible, s, NEG_INF), vn, *state[c])
        _diff_finish(lam_ref, gsub_ref, o_ref, m_ref, l_ref, acc_ref, n_heads, lam_init)


def _diff_sample(z, cache_k, cache_v, layer_slot, page_table, lam_p, g_sub, t, n_heads, dh, lam_init):
    m = z.shape[0]
    n_seq, n_pages = page_table.shape
    page2 = cache_k.shape[2]
    qw = n_heads * 2 * dh
    kcol = qw // (2 * dh)
    pages = _tile(n_pages, PAGES_PER_STEP, 1)
    chunk = _tile(pages, PAGES_PER_CHUNK, 1)
    n_steps = n_pages // pages
    rows = n_heads * t

    def page_spec(p):
        return pl.BlockSpec((None, None, page2, dh),
                            lambda s, g, pt: (layer_slot, pt[s, g * pages + p], 0, 0))

    grid_spec = pltpu.PrefetchScalarGridSpec(
        num_scalar_prefetch=1, grid=(n_seq, n_steps),
        in_specs=[pl.BlockSpec((4, dh), lambda s, g, pt: (0, 0)),
                  pl.BlockSpec((1, 2 * dh), lambda s, g, pt: (0, 0)),
                  pl.BlockSpec((t, qw), lambda s, g, pt: (s, 0)),
                  pl.BlockSpec((t, 2 * dh), lambda s, g, pt: (s, kcol)),
                  pl.BlockSpec((t, 2 * dh), lambda s, g, pt: (s, kcol + 1))]
                 + [page_spec(p) for p in range(pages)] * 2,
        out_specs=pl.BlockSpec((t, qw), lambda s, g, pt: (s, 0)),
        scratch_shapes=[pltpu.VMEM((2, rows, 1), F32), pltpu.VMEM((2, rows, 1), F32),
                        pltpu.VMEM((2, rows, 2 * dh), F32), pltpu.VMEM((2, rows, dh), BF16)])
    return pl.pallas_call(
        functools.partial(_diff_sample_kernel, n_heads=n_heads, dh=dh, lam_init=lam_init,
                          pages=pages, chunk=chunk, n_steps=n_steps),
        grid_spec=grid_spec, out_shape=jax.ShapeDtypeStruct((m, qw), F32),
        compiler_params=_params(2), name="diff_attn_sample",
    )(page_table, lam_p, g_sub.reshape(1, 2 * dh), z, z, z,
      *([cache_k] * pages), *([cache_v] * pages))


def _softcap(x):
    return GATE_CAP * jnp.tanh(x / GATE_CAP)


def _mlstm_kernel(*refs, n_heads, dqk, dv, n_chunks, has_init):
    if has_init:
        (gb_ref, q_ref, k_ref, v_ref, zo_ref, gates_ref, gout_ref, c0_ref, n0_ref, m0_ref,
         y_ref, co_ref, no_ref, mo_ref, c_scr, n_scr, m_scr) = refs
    else:
        (gb_ref, q_ref, k_ref, v_ref, zo_ref, gates_ref, gout_ref,
         y_ref, co_ref, no_ref, mo_ref, c_scr, n_scr, m_scr) = refs
    c = pl.program_id(1)

    @pl.when(c == 0)
    def _():
        if has_init:
            c_scr[...] = c0_ref[...]
            n_scr[...] = n0_ref[...]
            m_scr[...] = m0_ref[...]
        else:
            c_scr[...] = jnp.zeros(c_scr.shape, F32)
            n_scr[...] = jnp.zeros(n_scr.shape, F32)
            m_scr[...] = jnp.zeros(m_scr.shape, F32)

    ln = q_ref.shape[0]
    mm = BF16 if ln >= BF16_ROWS else F32
    row = lax.broadcasted_iota(jnp.int32, (ln, ln), 0)
    col = lax.broadcasted_iota(jnp.int32, (ln, ln), 1)
    causal = col <= row
    diag = col == row
    for h in range(n_heads):
        q = q_ref[:, h * dqk:(h + 1) * dqk]
        k = k_ref[:, h * dqk:(h + 1) * dqk] * (dqk ** -0.5)
        v = v_ref[:, h * dv:(h + 1) * dv]
        ig = _softcap(gates_ref[h] + gb_ref[0, h])
        lf = jax.nn.log_sigmoid(_softcap(gates_ref[n_heads + h] + gb_ref[1, h]))
        bcum_c = jnp.sum(jnp.where(causal, jnp.broadcast_to(lf, (ln, ln)), 0.0), axis=1, keepdims=True)
        bcum_r = jnp.sum(jnp.where(diag, jnp.broadcast_to(bcum_c, (ln, ln)), 0.0), axis=0, keepdims=True)
        ig_c = jnp.sum(jnp.where(diag, jnp.broadcast_to(ig, (ln, ln)), 0.0), axis=1, keepdims=True)
        m_prev = m_scr[h]
        log_w = jnp.where(causal, bcum_c - bcum_r + ig, -jnp.inf)
        log_inter = bcum_c + m_prev
        m_t = jnp.maximum(log_inter, jnp.max(log_w, axis=1, keepdims=True))
        qm, km, vm = q.astype(mm), k.astype(mm), v.astype(mm)
        s = _dot_nt(qm, km) * jnp.exp(log_w - m_t)
        a_inter = jnp.exp(log_inter - m_t)
        c_old = c_scr[h]
        n_old = n_scr[h]
        if ln >= BF16_ROWS:
            q_c = _dot(qm, c_old.astype(BF16))
        else:
            q_pad = jnp.concatenate([q, jnp.zeros((BF16_ROWS - ln, dqk), F32)], axis=0).astype(BF16)
            q_c = _dot(q_pad, c_old.astype(BF16))[:ln]
        num = _dot(s.astype(mm), vm) + a_inter * q_c
        den = jnp.sum(s, axis=1, keepdims=True) + a_inter * jnp.sum(q * n_old, axis=1, keepdims=True)
        hcell = num / jnp.maximum(jnp.abs(den), jnp.exp(-m_t))
        m_new = m_t[ln - 1:ln, :]
        b_last = bcum_c[ln - 1:ln, :]
        decay = jnp.exp(b_last + m_prev - m_new)
        w_in = jnp.exp(b_last - bcum_c + ig_c - m_new)
        kw = w_in * k
        c_scr[h] = decay * c_old + _dot_tn(kw.astype(mm), vm)
        n_scr[h] = decay * n_old + jnp.sum(kw, axis=0, keepdims=True)
        m_scr[h] = m_new
        y = _rms(hcell) * gout_ref[h] * jax.nn.sigmoid(zo_ref[:, h * dv:(h + 1) * dv])
        y_ref[:, h * dv:(h + 1) * dv] = y.astype(y_ref.dtype)

    @pl.when(c == n_chunks - 1)
    def _():
        co_ref[...] = c_scr[...]
        no_ref[...] = n_scr[...]
        mo_ref[...] = m_scr[...]


def _mlstm(z, zg, gate_bias, g_out, n_seq, t, n_heads, dqk, dv, init=None, out_dtype=BF16):
    m = z.shape[0]
    ln = _tile(t, MLSTM_CHUNK, SUBLANES)
    nc = t // ln
    gates = zg[:, :2 * n_heads].T.reshape(2 * n_heads, m // ln, 1, ln)
    wqk, wv = n_heads * dqk, n_heads * dv
    assert (2 * wqk) % wv == 0
    vcol = 2 * wqk // wv
    in_specs = [pl.BlockSpec(memory_space=pltpu.SMEM),
                pl.BlockSpec((ln, wqk), lambda b, c: (b * nc + c, 0)),
                pl.BlockSpec((ln, wqk), lambda b, c: (b * nc + c, 1)),
                pl.BlockSpec((ln, wv), lambda b, c: (b * nc + c, vcol)),
                pl.BlockSpec((ln, wv), lambda b, c: (b * nc + c, vcol + 1)),
                pl.BlockSpec((2 * n_heads, None, 1, ln), lambda b, c: (0, b * nc + c, 0, 0)),
                pl.BlockSpec((n_heads, 1, dv), lambda b, c: (0, 0, 0))]
    args = [gate_bias, z, z, z, z, gates, g_out.reshape(n_heads, 1, dv)]
    state_specs = [pl.BlockSpec((None, n_heads, dqk, dv), lambda b, c: (b, 0, 0, 0)),
                   pl.BlockSpec((None, n_heads, 1, dqk), lambda b, c: (b, 0, 0, 0)),
                   pl.BlockSpec((None, n_heads, 1, 1), lambda b, c: (b, 0, 0, 0))]
    if init is not None:
        c0, n0, m0 = init
        in_specs += state_specs
        args += [c0, n0.reshape(n_seq, n_heads, 1, dqk), m0.reshape(n_seq, n_heads, 1, 1)]
    y, c_new, n_new, m_new = pl.pallas_call(
        functools.partial(_mlstm_kernel, n_heads=n_heads, dqk=dqk, dv=dv, n_chunks=nc, has_init=init is not None),
        grid=(n_seq, nc), in_specs=in_specs,
        out_specs=[pl.BlockSpec((ln, wv), lambda b, c: (b * nc + c, 0))] + state_specs,
        out_shape=[jax.ShapeDtypeStruct((m, wv), out_dtype),
                   jax.ShapeDtypeStruct((n_seq, n_heads, dqk, dv), F32),
                   jax.ShapeDtypeStruct((n_seq, n_heads, 1, dqk), F32),
                   jax.ShapeDtypeStruct((n_seq, n_heads, 1, 1), F32)],
        scratch_shapes=[pltpu.VMEM((n_heads, dqk, dv), F32), pltpu.VMEM((n_heads, 1, dqk), F32),
                        pltpu.VMEM((n_heads, 1, 1), F32)],
        compiler_params=_params(2), name="mlstm",
    )(*args)
    return y, c_new, n_new.reshape(n_seq, n_heads, dqk), m_new.reshape(n_seq, n_heads)


def _mla_post_kernel(z_ref, z2_ref, gq_ref, gkv_ref, cos_ref, sin_ref, cq_ref, ckv_ref, kpe_ref, *, q_lora):
    z = z_ref[...]
    cq_ref[...] = (_rms(z[:, :q_lora]) * gq_ref[...]).astype(cq_ref.dtype)
    ckv_ref[...] = _rms(z[:, q_lora:]) * gkv_ref[...]
    z2 = z2_ref[...]
    r = z2.shape[1] // 2
    kpe_ref[...] = _pos_mul(z2[:, :r], cos_ref[...]) + _pos_mul(z2[:, r:], sin_ref[...])


def _mla_post(z, z2, g_q, g_kv, cos, sin, t):
    m, zc = z.shape
    q_lora, kv_lora, rope = g_q.shape[0], g_kv.shape[0], cos.shape[1]
    tm = _row_tile(m, t, NORM_ROW_TILE)
    return pl.pallas_call(
        functools.partial(_mla_post_kernel, q_lora=q_lora),
        grid=(m // tm, 1),
        in_specs=[pl.BlockSpec((tm, zc), lambda i, j: (i, 0)),
                  pl.BlockSpec((tm, 2 * rope), lambda i, j: (i, 0)),
                  pl.BlockSpec((1, q_lora), lambda i, j: (0, 0)),
                  pl.BlockSpec((1, kv_lora), lambda i, j: (0, 0)),
                  _pos_spec(cos, t, tm, rope), _pos_spec(sin, t, tm, rope)],
        out_specs=[pl.BlockSpec((tm, q_lora), lambda i, j: (i, 0)),
                   pl.BlockSpec((tm, kv_lora), lambda i, j: (i, 0)),
                   pl.BlockSpec((tm, rope), lambda i, j: (i, 0))],
        out_shape=[jax.ShapeDtypeStruct((m, q_lora), BF16),
                   jax.ShapeDtypeStruct((m, kv_lora), F32),
                   jax.ShapeDtypeStruct((m, rope), F32)],
        compiler_params=_params(2), name="mla_post",
    )(z, z2, g_q.reshape(1, q_lora), g_kv.reshape(1, kv_lora), cos, sin)


def _rope_linear_kernel(a_ref, w_ref, wrot_ref, cos_ref, sin_ref, o_ref):
    a = a_ref[...]
    x = _dot(a, w_ref[...].astype(BF16))
    x_rot = _dot(a, wrot_ref[...].astype(BF16))
    o_ref[...] = _pos_mul(x, cos_ref[...]) + _pos_mul(x_rot, sin_ref[...])


def _rope_linear(a, w, w_rot, cos, sin, t):
    m, kdim = a.shape
    n = w.shape[1]
    tm = _row_tile(m, t, ROW_TILE)
    tn = _tile(n, COL_TILE, 128)
    return pl.pallas_call(
        _rope_linear_kernel, grid=(m // tm, n // tn),
        in_specs=[pl.BlockSpec((tm, kdim), lambda i, j: (i, 0)),
                  pl.BlockSpec((kdim, tn), lambda i, j: (0, j)),
                  pl.BlockSpec((kdim, tn), lambda i, j: (0, j)),
                  _pos_spec(cos, t, tm, tn), _pos_spec(sin, t, tm, tn)],
        out_specs=pl.BlockSpec((tm, tn), lambda i, j: (i, j)),
        out_shape=jax.ShapeDtypeStruct((m, n), F32),
        compiler_params=_params(2), name="rope_linear",
    )(a, w, w_rot, cos, sin)


def _mla_prompt_kernel(qi_ref, kj_ref, last_ref, qn_ref, qp_ref, kv_ref, kpe_ref, o_ref, m_ref, l_ref, acc_ref,
                       *, hb, nope, rope, vd):
    pair = pl.program_id(2)
    qi, kj, last = qi_ref[pair], kj_ref[pair], last_ref[pair]
    tq, tk = qn_ref.shape[0], kv_ref.shape[0]
    scale = (nope + rope) ** -0.5 * LOG2E

    @pl.when(kj == 0)
    def _():
        _softmax_init(m_ref, l_ref, acc_ref)

    def run(n_keys, masked):
        kpe = kpe_ref[:n_keys, :].astype(BF16)
        if masked:
            visible = _causal_visible(qi, kj, tq, tk, tq, n_keys)
        for h in range(hb):
            q = jnp.concatenate([qn_ref[:, h * nope:(h + 1) * nope].astype(BF16),
                                 qp_ref[:, h * rope:(h + 1) * rope].astype(BF16)], axis=1)
            k = jnp.concatenate([kv_ref[:n_keys, h * (nope + vd): h * (nope + vd) + nope], kpe], axis=1)
            v = kv_ref[:n_keys, h * (nope + vd) + nope: (h + 1) * (nope + vd)]
            s = _dot_nt(q, k) * scale
            if masked:
                s = jnp.where(visible, s, NEG_INF)
            m_ref[h], l_ref[h], acc_ref[h] = _softmax_fold(s, v, m_ref[h], l_ref[h], acc_ref[h])

    @pl.when(last == 0)
    def _():
        run(tk, False)

    @pl.when(last == 1)
    def _():
        _for_each_causal_prefix(qi, tq, tk, lambda n_keys: run(n_keys, True))
        for h in range(hb):
            o_ref[:, h * vd:(h + 1) * vd] = (acc_ref[h] / l_ref[h]).astype(o_ref.dtype)


def _mla_prompt(qn, qp, kv, kpe, n_seq, t, n_heads, nope, rope, vd):
    m = qn.shape[0]
    tq = _tile(t, MLA_TQ, BF16_ROWS)
    tk = _tile(t, MLA_TK, tq)
    nq, nk = t // tq, t // tk
    hb = _tile(n_heads, MLA_HEAD_BLOCK, 2)
    tabs = _causal_pairs(nq, tq, tk)
    grid_spec = pltpu.PrefetchScalarGridSpec(
        num_scalar_prefetch=3, grid=(n_seq, n_heads // hb, tabs[0].shape[0]),
        in_specs=[pl.BlockSpec((tq, hb * nope), lambda b, g, p, qi, kj, la: (b * nq + qi[p], g)),
                  pl.BlockSpec((tq, hb * rope), lambda b, g, p, qi, kj, la: (b * nq + qi[p], g)),
                  pl.BlockSpec((tk, hb * (nope + vd)), lambda b, g, p, qi, kj, la: (b * nk + kj[p], g)),
                  pl.BlockSpec((tk, rope), lambda b, g, p, qi, kj, la: (b * nk + kj[p], 0))],
        out_specs=pl.BlockSpec((tq, hb * vd), lambda b, g, p, qi, kj, la: (b * nq + qi[p], g)),
        scratch_shapes=[pltpu.VMEM((hb, tq, 1), F32), pltpu.VMEM((hb, tq, 1), F32),
                        pltpu.VMEM((hb, tq, vd), F32)])
    return pl.pallas_call(
        functools.partial(_mla_prompt_kernel, hb=hb, nope=nope, rope=rope, vd=vd),
        grid_spec=grid_spec, out_shape=jax.ShapeDtypeStruct((m, n_heads * vd), BF16),
        compiler_params=_params(3), name="mla_attn_prompt",
    )(*tabs, qn, qp, kv, kpe)


def _q_latent_kernel(qn_ref, wuk_ref, o_ref):
    x = _dot_nt(qn_ref[...].astype(BF16), wuk_ref[...].astype(BF16))
    o_ref[...] = x.reshape(o_ref.shape)


def _q_latent(qn, w_kvb2, n_seq, t, n_heads, nope, vd):
    m = qn.shape[0]
    kvl = w_kvb2.shape[0]
    assert nope == vd
    return pl.pallas_call(
        _q_latent_kernel, grid=(n_heads,),
        in_specs=[pl.BlockSpec((m, nope), lambda h: (0, h)),
                  pl.BlockSpec((kvl, nope), lambda h: (0, 2 * h))],
        out_specs=pl.BlockSpec((n_seq, None, t, kvl), lambda h: (0, h, 0, 0)),
        out_shape=jax.ShapeDtypeStruct((n_seq, n_heads, t, kvl), F32),
        compiler_params=_params(1), name="mla_q_latent",
    )(qn, w_kvb2)


def _v_up_kernel(ol_ref, wuv_ref, o_ref):
    n_seq, t, kvl = ol_ref.shape
    x = ol_ref[...].reshape(n_seq * t, kvl).astype(BF16)
    o_ref[...] = _dot(x, wuv_ref[...].astype(BF16))


def _v_up(o_lat, w_kvb2, nope, vd):
    n_seq, n_heads, t, kvl = o_lat.shape
    assert nope == vd
    return pl.pallas_call(
        _v_up_kernel, grid=(n_heads,),
        in_specs=[pl.BlockSpec((n_seq, None, t, kvl), lambda h: (0, h, 0, 0)),
                  pl.BlockSpec((kvl, vd), lambda h: (0, 2 * h + 1))],
        out_specs=pl.BlockSpec((n_seq * t, vd), lambda h: (0, h)),
        out_shape=jax.ShapeDtypeStruct((n_seq * t, n_heads * vd), F32),
        compiler_params=_params(1), name="mla_v_up",
    )(o_lat, w_kvb2)


def _mla_sample_kernel(pt_ref, ql_ref, qp_ref, cn_ref, rn_ref, *rest,
                       n_heads, rope, pages, chunk, n_steps, scale):
    del pt_ref
    cp_refs, rp_refs = rest[:pages], rest[pages:2 * pages]
    o_ref, m_ref, l_ref, acc_ref, ql_scr, qp_scr = rest[2 * pages:]
    g = pl.program_id(1)
    _, t, kvl = ql_ref.shape
    rows = n_heads * t

    @pl.when(g == 0)
    def _():
        _softmax_init(m_ref, l_ref, acc_ref)
        ql_scr[...] = ql_ref[...].reshape(rows, kvl).astype(BF16)
        qp = qp_ref[...]
        qp_scr[...] = jnp.concatenate([qp[:, h * rope:(h + 1) * rope] for h in range(n_heads)],
                                      axis=0).astype(BF16)

    q_lat, q_pe = ql_scr[...], qp_scr[...]
    parts = []
    for c0 in range(0, pages, chunk):
        ckv = jnp.concatenate([r[...] for r in cp_refs[c0:c0 + chunk]], axis=0).astype(BF16)
        kpe_t = jnp.concatenate([r[...] for r in rp_refs[c0:c0 + chunk]], axis=1).astype(BF16)
        s = (_dot_nt(q_lat, ckv) + _dot(q_pe, kpe_t)) * scale
        parts.append(_softmax_local(s, ckv))
    state = _softmax_merge((m_ref[0], l_ref[0], acc_ref[0]), parts)

    @pl.when(g < n_steps - 1)
    def _():
        m_ref[0], l_ref[0], acc_ref[0] = state

    @pl.when(g == n_steps - 1)
    def _():
        page = cp_refs[0].shape[0]
        cn = _pad_keys(cn_ref[...], page).astype(BF16)
        rn = _pad_keys(rn_ref[...], page).astype(BF16)
        sn = (_dot_nt(q_lat, cn) + _dot_nt(q_pe, rn)) * scale
        _, l_fin, acc_fin = _softmax_fold(jnp.where(_new_key_mask(rows, t, page), sn, NEG_INF), cn, *state)
        o_ref[...] = (acc_fin / l_fin).reshape(o_ref.shape)


def _mla_sample(q_lat, qp, ckv_new, kpe_new, cache_ckv, cache_kpe_t, layer_slot, page_table, nope):
    n_seq, n_heads, t, kvl = q_lat.shape
    rope = kpe_new.shape[1]
    n_pages = page_table.shape[1]
    page = cache_ckv.shape[2]
    pages = _tile(n_pages, MLA_PAGES_PER_STEP, 1)
    chunk = _tile(pages, PAGES_PER_CHUNK, 1)
    n_steps = n_pages // pages
    rows = n_heads * t

    def page_spec(blk, p):
        return pl.BlockSpec((None, None) + blk, lambda s, g, pt: (layer_slot, pt[s, g * pages + p], 0, 0))

    grid_spec = pltpu.PrefetchScalarGridSpec(
        num_scalar_prefetch=1, grid=(n_seq, n_steps),
        in_specs=[pl.BlockSpec((None, n_heads, t, kvl), lambda s, g, pt: (s, 0, 0, 0)),
                  pl.BlockSpec((t, n_heads * rope), lambda s, g, pt: (s, 0)),
                  pl.BlockSpec((t, kvl), lambda s, g, pt: (s, 0)),
                  pl.BlockSpec((t, rope), lambda s, g, pt: (s, 0))]
                 + [page_spec((page, kvl), p) for p in range(pages)]
                 + [page_spec((rope, page), p) for p in range(pages)],
        out_specs=pl.BlockSpec((None, n_heads, t, kvl), lambda s, g, pt: (s, 0, 0, 0)),
        scratch_shapes=[pltpu.VMEM((1, rows, 1), F32), pltpu.VMEM((1, rows, 1), F32),
                        pltpu.VMEM((1, rows, kvl), F32),
                        pltpu.VMEM((rows, kvl), BF16), pltpu.VMEM((rows, rope), BF16)])
    return pl.pallas_call(
        functools.partial(_mla_sample_kernel, n_heads=n_heads, rope=rope, pages=pages, chunk=chunk,
                          n_steps=n_steps, scale=(nope + rope) ** -0.5 * LOG2E),
        grid_spec=grid_spec, out_shape=jax.ShapeDtypeStruct((n_seq, n_heads, t, kvl), F32),
        compiler_params=_params(2), name="mla_attn_sample",
    )(page_table, q_lat, qp, ckv_new, kpe_new, *([cache_ckv] * pages), *([cache_kpe_t] * pages))


def _rotate_half_cols(w, rope):
    k, n = w.shape
    w3 = w.reshape(k, n // rope, rope)
    half = rope // 2
    return jnp.concatenate([-w3[..., half:], w3[..., :half]], axis=-1).reshape(k, n)


def _rope_tables(pos, rope, repeat):
    half = rope // 2
    freqs = ROPE_THETA ** (-jnp.arange(half, dtype=F32) / half)
    ang = pos.astype(F32)[:, None] * freqs[None, :]
    cos = jnp.tile(jnp.cos(ang), (1, 2 * repeat))
    sin = jnp.tile(jnp.sin(ang), (1, 2 * repeat))
    return cos, sin


def _diff_lambda_init(layer):
    return 0.8 - 0.6 * math.exp(-0.3 * layer)


def kernel(x_prompt, x_sample, cache_k_a, cache_v_a, state_C_b, state_n_b, state_m_b, cache_ckv_c, cache_kpe_c,
           state_ffn_conv, page_table, c_prompt, c_sample, w_ada, b_ada, ada_table, g_mix, g_ffn, g_final,
           a_w_qkv, a_lambda, a_g_sub, a_w_o, b_w_in, b_gate_bias, b_g_out, b_w_out,
           c_w_in, c_g_q, c_g_kv, c_w_qb, c_w_kvb, c_w_o, f_w_up, f_conv_w, f_conv_b, f_w_down):
    nb, tp, d = x_prompt.shape
    ns, ts, _ = x_sample.shape
    depth = g_mix.shape[0]
    page = cache_k_a.shape[2]
    past_len = page_table.shape[1] * page
    assert ts == SUBLANES and f_conv_w.shape[1] == CONV_W

    dh = a_lambda.shape[-1]
    a_heads = a_w_o.shape[1] // (2 * dh)
    b_heads, dqk, dv = state_C_b.shape[2:]
    kvl, rope, q_lora = cache_ckv_c.shape[-1], cache_kpe_c.shape[-1], c_g_q.shape[-1]
    c_heads = c_w_kvb.shape[2]
    nope = c_w_qb.shape[2] // c_heads - rope
    vd = c_w_kvb.shape[3] - nope
    d_ff = f_conv_w.shape[-1]
    f_pad = -(-d_ff // FFN_PAD) * FFN_PAD

    groups = (("p", nb, tp), ("s", ns, ts))
    x = {"p": x_prompt.reshape(nb * tp, d), "s": x_sample.reshape(ns * ts, d)}

    pad = -nb % SUBLANES
    c_all = jnp.concatenate([c_sample, c_prompt, jnp.zeros((pad, d), F32)], axis=0)
    mod_s, mod_p = _ada(c_all, w_ada, b_ada, ada_table, ns)
    mods = {"p": mod_p[:, :nb].reshape(depth, nb, 1, -1), "s": mod_s}

    pos = {"p": jnp.arange(tp, dtype=jnp.int32), "s": past_len + jnp.arange(ts, dtype=jnp.int32)}
    cache_k = cache_k_a.reshape(cache_k_a.shape[:2] + (2 * page, dh))
    cache_v = cache_v_a.reshape(cache_v_a.shape[:2] + (2 * page, dh))
    cache_kpe_t = jnp.swapaxes(cache_kpe_c, 2, 3)

    outs = {key: {"p": [], "s": []} for key in ("ka", "va", "cb", "nb", "mb", "ckv", "kpe", "cv")}

    for l in range(depth):
        kind, slot = l % N_MIXERS, l // N_MIXERS
        w_down = _cast_pad_rows(f_w_down, l, f_pad)
        if kind == 1:
            n_main = 2 * b_heads * (dqk + dv)
            w_gate = jnp.pad(b_w_in[slot][:, n_main:], ((0, 0), (0, 128 - 2 * b_heads)))
        elif kind == 2:
            w_kpe = c_w_in[slot][:, q_lora + kvl:]
            w_kpe2 = jnp.concatenate([w_kpe, _rotate_half_cols(w_kpe, rope)], axis=1)
            wq3 = c_w_qb[slot].reshape(q_lora, c_heads, nope + rope)
            w_qn = wq3[:, :, :nope].reshape(q_lora, c_heads * nope)
            w_qr = wq3[:, :, nope:].reshape(q_lora, c_heads * rope)
            w_qr_rot = _rotate_half_cols(w_qr, rope)
            w_kvb2 = c_w_kvb[slot].reshape(kvl, c_heads * (nope + vd))

        for key, n_seq, t in groups:
            mod = mods[key]
            xg = x[key]
            h = _norm(xg, g_mix, l, t, mod, 0, 1)
            if kind == 0:
                lam_init = _diff_lambda_init(l)
                z = _linear(h, a_w_qkv, t, layer=slot)
                qw = a_heads * 2 * dh
                k_new, v_new = z[:, qw:qw + 2 * dh], z[:, qw + 2 * dh:]
                outs["ka"][key].append(k_new.reshape(n_seq, t, 1, 2 * dh))
                outs["va"][key].append(v_new.reshape(n_seq, t, 1, 2 * dh))
                if key == "p":
                    o = _diff_prompt(z, a_lambda[slot], a_g_sub[slot], n_seq, t, a_heads, dh, lam_init)
                else:
                    o = _diff_sample(z, cache_k, cache_v, slot, page_table, a_lambda[slot], a_g_sub[slot],
                                     t, a_heads, dh, lam_init).astype(BF16)
                xg = _linear(o, a_w_o, t, layer=slot, residual=(xg, mod, l, 2))
            elif kind == 1:
                z = _linear(h, b_w_in, t, layer=slot, n_out=n_main)
                zg = _linear(h, w_gate, t)
                if key == "p":
                    y, c_new, n_new, m_new = _mlstm(z, zg, b_gate_bias[slot], b_g_out[slot], n_seq, t,
                                                    b_heads, dqk, dv)
                else:
                    y, c_new, n_new, m_new = _mlstm(z, zg, b_gate_bias[slot], b_g_out[slot], n_seq, t,
                                                    b_heads, dqk, dv, out_dtype=F32,
                                                    init=(state_C_b[slot], state_n_b[slot], state_m_b[slot]))
                    y = y.astype(BF16)
                outs["cb"][key].append(c_new)
                outs["nb"][key].append(n_new)
                outs["mb"][key].append(m_new)
                xg = _linear(y, b_w_out, t, layer=slot, residual=(xg, mod, l, 2))
            else:
                z = _linear(h, c_w_in, t, layer=slot, n_out=q_lora + kvl)
                z2 = _linear(h, w_kpe2, t)
                cos1, sin1 = _rope_tables(pos[key], rope, 1)
                cq, ckv, kpe = _mla_post(z, z2, c_g_q[slot], c_g_kv[slot], cos1, sin1, t)
                outs["ckv"][key].append(ckv.reshape(n_seq, t, kvl))
                outs["kpe"][key].append(kpe.reshape(n_seq, t, rope))
                cosh, sinh = _rope_tables(pos[key], rope, c_heads)
                qn = _linear(cq, w_qn, t, tn=DOWN_COL_TILE)
                qp = _rope_linear(cq, w_qr, w_qr_rot, cosh, sinh, t)
                if key == "p":
                    kv = _linear(ckv, w_kvb2, t, tn=KV_UP_COL_TILE, out_dtype=BF16)
                    o = _mla_prompt(qn, qp, kv, kpe, n_seq, t, c_heads, nope, rope, vd)
                else:
                    q_lat = _q_latent(qn, w_kvb2, n_seq, t, c_heads, nope, vd)
                    o_lat = _mla_sample(q_lat, qp, ckv, kpe, cache_ckv_c, cache_kpe_t, slot, page_table, nope)
                    o = _v_up(o_lat, w_kvb2, nope, vd).astype(BF16)
                xg = _linear(o, c_w_o, t, layer=slot, k_steps=2, residual=(xg, mod, l, 2))

            h = _norm(xg, g_ffn, l, t, mod, 3, 4)
            prev = state_ffn_conv[l] if key == "s" else None
            g, tail = _ffn_up(h, f_w_up, f_conv_w, f_conv_b, l, t, prev)
            outs["cv"][key].append(tail.reshape(n_seq, SUBLANES, -1)[:, SUBLANES - (CONV_W - 1):, :d_ff])
            x[key] = _linear(g, w_down, t, tn=DOWN_COL_TILE, k_steps=FFN_DOWN_K_STEPS,
                             residual=(xg, mod, l, 5))

    y_p = _norm(x["p"], g_final, 0, tp, out_dtype=F32).reshape(nb, tp, d)
    y_s = _norm(x["s"], g_final, 0, ts, out_dtype=F32).reshape(ns, ts, d)
    res = [y_p, y_s]
    for key in ("ka", "va", "cb", "nb", "mb", "ckv", "kpe", "cv"):
        res += [jnp.stack(outs[key]["p"]), jnp.stack(outs[key]["s"])]
    return tuple(res)
```
